```python
import math
import jax, jax.numpy as jnp
from jax import lax
import numpy as np

D_MODEL = 2048
BATCH = 4
SEQ = 4096
DEPTH = 4

CHUNK = 64
Q_BLOCK = 128
D_MIX = D_MODEL
D_POOL = D_MIX // 2
D_DIFF = D_MIX - D_POOL
POOL_WINDOWS = (2, 4, 8, 16)
N_POOL_GROUPS = len(POOL_WINDOWS)
POOL_GROUP_DIM = D_POOL // N_POOL_GROUPS
DIFF_HEADS = 8
DIFF_V_DIM = D_DIFF // DIFF_HEADS
DIFF_QK_DIM = DIFF_V_DIM // 2
D_QK = DIFF_HEADS * 2 * DIFF_QK_DIM
D_IN = 2 * D_POOL + 2 * D_QK + 2 * D_DIFF
ROPE_THETA = 10000.0
EPS = 1e-6
NEG_INF = -1e30

kernel_name = "hybrid_pool_diffattn_sandwich_adaln"


def rms_norm(x, g):
    xf = x.astype(jnp.float32)
    y = xf * lax.rsqrt(jnp.mean(xf * xf, axis=-1, keepdims=True) + EPS)
    return (y * g.astype(jnp.float32)).astype(x.dtype)


def rope_tables(positions):
    inv_freq = 1.0 / (ROPE_THETA ** (jnp.arange(0, DIFF_QK_DIM, 2, dtype=jnp.float32) / DIFF_QK_DIM))
    ang = positions.astype(jnp.float32)[..., None] * inv_freq
    emb = jnp.concatenate([ang, ang], axis=-1)[:, :, None, None, :]
    return jnp.cos(emb), jnp.sin(emb)


def apply_rope(x, cos, sin):
    xf = x.astype(jnp.float32)
    half = DIFF_QK_DIM // 2
    rot = jnp.concatenate([-xf[..., half:], xf[..., :half]], axis=-1)
    return (xf * cos + rot * sin).astype(x.dtype)


def multiscale_pool(u, w_pool, pool_scale):
    B, S, _ = u.shape
    ug = u.reshape(B, S, N_POOL_GROUPS, POOL_GROUP_DIM).astype(jnp.float32)
    t1 = jnp.arange(1, S + 1, dtype=jnp.int32)
    outs = []
    for g, w in enumerate(POOL_WINDOWS):
        xg = ug[:, :, g]
        cs = jnp.cumsum(xg, axis=1)
        cs_lag = jnp.pad(cs, ((0, 0), (w, 0), (0, 0)))[:, :S]
        count = jnp.minimum(t1, w).astype(jnp.float32)[None, :, None]
        outs.append((cs - cs_lag) / count - xg)
    pooled = jnp.stack(outs, axis=2)
    mixed = jnp.einsum('bsgc,gcd->bsgd', pooled, w_pool.astype(jnp.float32))
    mixed = mixed * pool_scale.astype(jnp.float32).reshape(N_POOL_GROUPS, POOL_GROUP_DIM)
    return mixed.reshape(B, S, D_POOL).astype(u.dtype)


def diff_attention(q, k, v, lam):
    B, S = q.shape[0], q.shape[1]
    nb = S // Q_BLOCK
    scale = 1.0 / math.sqrt(DIFF_QK_DIM)
    q_blocks = jnp.moveaxis(q.reshape(B, nb, Q_BLOCK, DIFF_HEADS, 2, DIFF_QK_DIM), 1, 0)
    key_chunk = jnp.arange(S, dtype=jnp.int32) // CHUNK
    vf = v.astype(jnp.float32)
    kf = k.astype(jnp.float32)

    def block(args):
        qblk, bidx = args
        q_chunk = (bidx * Q_BLOCK + jnp.arange(Q_BLOCK, dtype=jnp.int32)) // CHUNK
        mask = key_chunk[None, :] <= q_chunk[:, None]
        s = jnp.einsum('bqhmd,bkhmd->bhmqk', qblk.astype(jnp.float32), kf) * scale
        s = jnp.where(mask[None, None, None], s, NEG_INF)
        p = jax.nn.softmax(s, axis=-1)
        a = p[:, :, 0] - lam * p[:, :, 1]
        return jnp.einsum('bhqk,bkhd->bqhd', a, vf)

    out = lax.map(block, (q_blocks, jnp.arange(nb, dtype=jnp.int32)))
    out = jnp.moveaxis(out, 0, 1).reshape(B, S, DIFF_HEADS, DIFF_V_DIM)
    return out.astype(v.dtype)


def hybrid_layer(x, c, cos, sin, layer_idx, w_ada, b_ada, g_pre, w_in, w_pool, pool_scale,
                 lq1, lk1, lq2, lk2, subln_g, w_out, g_post):
    B, S, _ = x.shape
    mod = (c @ w_ada + b_ada)[:, None, :]
    shift, scale, gate = jnp.split(mod, 3, axis=-1)
    h = rms_norm(x, g_pre) * (1 + scale) + shift

    z = h @ w_in
    splits = np.cumsum([D_POOL, D_POOL, D_QK, D_QK, D_DIFF]).tolist()
    u, g_pool, q, k, v, g_diff = jnp.split(z, splits, axis=-1)

    pool_out = multiscale_pool(u, w_pool, pool_scale) * jax.nn.silu(g_pool)

    lam_init = 0.8 - 0.6 * math.exp(-0.3 * layer_idx)
    lam = (jnp.exp(jnp.sum(lq1.astype(jnp.float32) * lk1.astype(jnp.float32)))
           - jnp.exp(jnp.sum(lq2.astype(jnp.float32) * lk2.astype(jnp.float32))) + lam_init)
    q = apply_rope(q.reshape(B, S, DIFF_HEADS, 2, DIFF_QK_DIM), cos, sin)
    k = apply_rope(k.reshape(B, S, DIFF_HEADS, 2, DIFF_QK_DIM), cos, sin)
    v = v.reshape(B, S, DIFF_HEADS, DIFF_V_DIM)
    att = diff_attention(q, k, v, lam)
    att = rms_norm(att, subln_g) * (1.0 - lam_init)
    diff_out = att.reshape(B, S, D_DIFF) * jax.nn.silu(g_diff)

    y = jnp.concatenate([pool_out, diff_out], axis=-1) @ w_out
    return x + (1 + gate) * rms_norm(y, g_post)


def setup_inputs(seed: int = 0) -> dict:
    key = jax.random.key(seed)
    ks = jax.random.split(key, 20)
    f32 = jnp.float32
    nrm = lambda k, shape, s: jax.random.normal(k, shape, f32) * s
    x = jax.random.normal(ks[0], (BATCH, SEQ, D_MODEL), f32)
    c = jax.random.normal(ks[1], (BATCH, D_MODEL), f32)
    offset = jax.random.randint(ks[2], (BATCH, 1), 0, 8192, dtype=jnp.int32)
    positions = offset + jnp.arange(SEQ, dtype=jnp.int32)[None, :]
    return {
        "x": x,
        "c": c,
        "positions": positions,
        "w_ada": nrm(ks[3], (DEPTH, D_MODEL, 3 * D_MODEL), 0.1 * D_MODEL ** -0.5),
        "b_ada": nrm(ks[4], (DEPTH, 3 * D_MODEL), 0.01),
        "g_pre": 1.0 + nrm(ks[5], (DEPTH, D_MODEL), 0.02),
        "w_in": nrm(ks[6], (DEPTH, D_MODEL, D_IN), D_MODEL ** -0.5),
        "w_pool": nrm(ks[7], (DEPTH, N_POOL_GROUPS, POOL_GROUP_DIM, POOL_GROUP_DIM), POOL_GROUP_DIM ** -0.5),
        "pool_scale": 1.0 + nrm(ks[8], (DEPTH, D_POOL), 0.02),
        "lambda_q1": nrm(ks[9], (DEPTH, DIFF_QK_DIM), 0.1),
        "lambda_k1": nrm(ks[10], (DEPTH, DIFF_QK_DIM), 0.1),
        "lambda_q2": nrm(ks[11], (DEPTH, DIFF_QK_DIM), 0.1),
        "lambda_k2": nrm(ks[12], (DEPTH, DIFF_QK_DIM), 0.1),
        "subln_g": 1.0 + nrm(ks[13], (DEPTH, DIFF_V_DIM), 0.02),
        "w_out": nrm(ks[14], (DEPTH, D_MIX, D_MODEL), D_MIX ** -0.5),
        "g_post": 1.0 + nrm(ks[15], (DEPTH, D_MODEL), 0.02),
    }


def reference(x, c, positions, w_ada, b_ada, g_pre, w_in, w_pool, pool_scale,
              lambda_q1, lambda_k1, lambda_q2, lambda_k2, subln_g, w_out, g_post):
    cos, sin = rope_tables(positions)
    for l in range(DEPTH):
        x = hybrid_layer(x, c, cos, sin, l, w_ada[l], b_ada[l], g_pre[l], w_in[l], w_pool[l],
                         pool_scale[l], lambda_q1[l], lambda_k1[l], lambda_q2[l], lambda_k2[l],
                         subln_g[l], w_out[l], g_post[l])
    return x
```

```python
import functools
import math

import jax
import jax.numpy as jnp
from jax import lax
from jax.experimental import pallas as pl
from jax.experimental.pallas import tpu as pltpu

D_MODEL = 2048
CHUNK = 64
D_POOL = 1024
D_DIFF = 1024
POOL_WINDOWS = (2, 4, 8, 16)
N_POOL_GROUPS = 4
POOL_GROUP_DIM = 256
DIFF_HEADS = 8
DIFF_V_DIM = 128
DIFF_QK_DIM = 64
D_QK = 1024
D_IN = 6144
ROPE_THETA = 10000.0
EPS = 1e-6
NEG_INF = -1e30

LANES = 128
POOL_HALO = 16
VMEM_LIMIT = 56 * 1024 * 1024

F32 = jnp.float32
BF16 = jnp.bfloat16


def _silu(g):
    return g * jax.nn.sigmoid(g)


def _rope_table_kernel(pos_ref, freq_ref, sign_ref, cos_ref, sin_ref):
    ang = pos_ref[0].astype(F32) * freq_ref[...]
    cos_ref[0] = jnp.cos(ang)
    sin_ref[0] = jnp.sin(ang) * sign_ref[...]


def _rope_tables(positions):
    B, S = positions.shape
    ts = 512
    inv_freq = 1.0 / (ROPE_THETA ** (jnp.arange(0, DIFF_QK_DIM, 2, dtype=F32) / DIFF_QK_DIM))
    freq = jnp.tile(inv_freq, LANES // (DIFF_QK_DIM // 2))[None, :]
    half = DIFF_QK_DIM // 2
    sign = jnp.where((jnp.arange(LANES) % DIFF_QK_DIM) < half, -1.0, 1.0).astype(F32)[None, :]
    cos, sin = pl.pallas_call(
        _rope_table_kernel,
        grid=(B, S // ts),
        in_specs=[
            pl.BlockSpec((1, ts, 1), lambda b, i: (b, i, 0)),
            pl.BlockSpec((1, LANES), lambda b, i: (0, 0)),
            pl.BlockSpec((1, LANES), lambda b, i: (0, 0)),
        ],
        out_specs=[
            pl.BlockSpec((1, ts, LANES), lambda b, i: (b, i, 0)),
            pl.BlockSpec((1, ts, LANES), lambda b, i: (b, i, 0)),
        ],
        out_shape=[jax.ShapeDtypeStruct((B, S, LANES), F32)] * 2,
        name="rope_tables",
    )(positions.reshape(B, S, 1), freq, sign)
    return cos.reshape(B * S, LANES), sin.reshape(B * S, LANES)


def _ada_kernel(c_ref, w_ref, b_ref, o_ref):
    o_ref[0] = jnp.dot(c_ref[...], w_ref[0], preferred_element_type=F32) + b_ref[0]


def _ada_modulation(c, w_ada, b_ada):
    depth, d, n = w_ada.shape
    B = c.shape[0]
    rows = 8
    c_pad = jnp.zeros((rows, d), F32).at[:B].set(c)
    tn = 1024
    out = pl.pallas_call(
        _ada_kernel,
        grid=(depth, n // tn),
        in_specs=[
            pl.BlockSpec((rows, d), lambda l, j: (0, 0)),
            pl.BlockSpec((1, d, tn), lambda l, j: (l, 0, j)),
            pl.BlockSpec((1, 1, tn), lambda l, j: (l, 0, j)),
        ],
        out_specs=pl.BlockSpec((1, rows, tn), lambda l, j: (l, 0, j)),
        out_shape=jax.ShapeDtypeStruct((depth, rows, n), F32),
        compiler_params=pltpu.CompilerParams(vmem_limit_bytes=VMEM_LIMIT),
        name="ada_modulation",
    )(c_pad, w_ada, b_ada.reshape(depth, 1, n))
    return out[:, :B]


def _in_proj_kernel(x_ref, shift_ref, scale_ref, gpre_ref, w_ref, cos_ref, sin_ref,
                    z_ref, h_ref, *, tm, tn):
    j = pl.program_id(1)

    @pl.when(j == 0)
    def _():
        x = x_ref[...]
        ms = jnp.mean(x * x, axis=-1, keepdims=True)
        y = x * lax.rsqrt(ms + EPS) * gpre_ref[...]
        h_ref[...] = (y * (1.0 + scale_ref[0]) + shift_ref[0]).astype(BF16)

    acc = jnp.dot(h_ref[...], w_ref[...], preferred_element_type=F32)
    q_tile = (2 * D_POOL) // tn
    k_tile = (2 * D_POOL + D_QK) // tn
    is_rope = jnp.logical_or(j == q_tile, j == k_tile)

    @pl.when(is_rope)
    def _():
        qk_scale = jnp.where(j == q_tile, 1.0 / math.sqrt(DIFF_QK_DIM), 1.0).astype(F32)
        cos = cos_ref[...] * qk_scale
        sin = sin_ref[...] * qk_scale
        lane = lax.broadcasted_iota(jnp.int32, (tm, LANES), 1)
        lower = (lane % DIFF_QK_DIM) < (DIFF_QK_DIM // 2)
        for g in range(tn // LANES):
            xg = acc[:, g * LANES:(g + 1) * LANES]
            rot = jnp.where(lower,
                            pltpu.roll(xg, LANES - DIFF_QK_DIM // 2, 1),
                            pltpu.roll(xg, DIFF_QK_DIM // 2, 1))
            z_ref[:, g * LANES:(g + 1) * LANES] = (xg * cos + rot * sin).astype(BF16)

    @pl.when(jnp.logical_not(is_rope))
    def _():
        z_ref[...] = acc.astype(BF16)


def _in_proj(x2, mod3, g_pre, w_in_bf, cos, sin, *, seq):
    M, D = x2.shape
    N = w_in_bf.shape[1]
    tm, tn = 1024, 1024
    per_seq = seq // tm
    kern = functools.partial(_in_proj_kernel, tm=tm, tn=tn)
    return pl.pallas_call(
        kern,
        grid=(M // tm, N // tn),
        in_specs=[
            pl.BlockSpec((tm, D), lambda i, j: (i, 0)),
            pl.BlockSpec((1, 1, D), lambda i, j: (i // per_seq, 0, 0)),
            pl.BlockSpec((1, 1, D), lambda i, j: (i // per_seq, 0, 1)),
            pl.BlockSpec((1, D), lambda i, j: (0, 0)),
            pl.BlockSpec((D, tn), lambda i, j: (0, j)),
            pl.BlockSpec((tm, LANES), lambda i, j: (i, 0)),
            pl.BlockSpec((tm, LANES), lambda i, j: (i, 0)),
        ],
        out_specs=pl.BlockSpec((tm, tn), lambda i, j: (i, j)),
        out_shape=jax.ShapeDtypeStruct((M, N), BF16),
        scratch_shapes=[pltpu.VMEM((tm, D), BF16)],
        compiler_params=pltpu.CompilerParams(
            dimension_semantics=("parallel", "arbitrary"), vmem_limit_bytes=VMEM_LIMIT),
        name="in_proj",
    )(x2, mod3, mod3, g_pre.reshape(1, D), w_in_bf, cos, sin)


def _attn_kernel(lq1_ref, lk1_ref, lq2_ref, lk2_ref, subg_ref, q_ref, k_ref, v_ref, g_ref,
                 o_ref, vt_ref, acc_ref, *, seq, tq, lam_init):
    tk = tq
    lam = (jnp.exp(jnp.sum(lq1_ref[...] * lk1_ref[...], axis=-1, keepdims=True))
           - jnp.exp(jnp.sum(lq2_ref[...] * lk2_ref[...], axis=-1, keepdims=True))
           + lam_init)

    for c in range(seq // tk):
        vt_ref[c] = v_ref[c * tk:(c + 1) * tk, :].astype(F32).T.astype(BF16)

    lane = lax.broadcasted_iota(jnp.int32, (tq, LANES), 1)
    first_map = lane < DIFF_QK_DIM
    key_chunk = lax.broadcasted_iota(jnp.int32, (tk, 2 * tq), 0) // CHUNK
    q_chunk = (lax.broadcasted_iota(jnp.int32, (tk, 2 * tq), 1) % tq) // CHUNK
    diag_mask = key_chunk <= q_chunk
    contract_last = (((1,), (1,)), ((), ()))

    def q_block(i, carry):
        q0 = pl.multiple_of(i * tq, tq)
        q = q_ref[pl.ds(q0, tq), :]
        zero = jnp.zeros_like(q)
        qcat = jnp.concatenate([jnp.where(first_map, q, zero),
                                jnp.where(first_map, zero, q)], axis=0)
        acc_ref[...] = jnp.zeros_like(acc_ref)

        def kv_step(j, ml, masked):
            m, l = ml
            k0 = pl.multiple_of(j * tk, tk)
            s = lax.dot_general(k_ref[pl.ds(k0, tk), :], qcat, contract_last,
                                preferred_element_type=F32)
            if masked:
                s = jnp.where(diag_mask, s, NEG_INF)
            m_new = jnp.maximum(m, jnp.max(s, axis=0, keepdims=True))
            alpha = jnp.exp(m - m_new)
            p = jnp.exp(s - m_new)
            l_new = alpha * l + jnp.sum(p, axis=0, keepdims=True)
            pv = jnp.dot(vt_ref[j], p.astype(BF16), preferred_element_type=F32)
            acc_ref[...] = acc_ref[...] * alpha + pv
            return m_new, l_new

        ml0 = (jnp.full((1, 2 * tq), NEG_INF, F32), jnp.zeros((1, 2 * tq), F32))
        ml = lax.fori_loop(0, i, lambda j, ml: kv_step(j, ml, False), ml0)
        _, l = kv_step(i, ml, True)

        o = acc_ref[...] / l
        att = (o[:, :tq] - lam * o[:, tq:]).T
        ms = jnp.mean(att * att, axis=-1, keepdims=True)
        y = att * lax.rsqrt(ms + EPS) * subg_ref[...] * (1.0 - lam_init)
        g = g_ref[pl.ds(q0, tq), :].astype(F32)
        o_ref[pl.ds(q0, tq), :] = (y * _silu(g)).astype(BF16)
        return carry

    lax.fori_loop(0, seq // tq, q_block, 0)


def _diff_attention(z3, lq1, lk1, lq2, lk2, subln_g, *, lam_init):
    B, S, _ = z3.shape
    tq = 256
    q_col = (2 * D_POOL) // LANES
    k_col = q_col + D_QK // LANES
    v_col = k_col + D_QK // LANES
    g_col = v_col + D_DIFF // LANES
    vec = lambda n: pl.BlockSpec((1, n), lambda b, h: (0, 0))
    head = lambda col: pl.BlockSpec((None, S, LANES), lambda b, h: (b, 0, col + h))
    kern = functools.partial(_attn_kernel, seq=S, tq=tq, lam_init=lam_init)
    return pl.pallas_call(
        kern,
        grid=(B, DIFF_HEADS),
        in_specs=[vec(DIFF_QK_DIM)] * 4 + [vec(DIFF_V_DIM),
                  head(q_col), head(k_col), head(v_col), head(g_col)],
        out_specs=pl.BlockSpec((None, S, LANES), lambda b, h: (b, 0, h)),
        out_shape=jax.ShapeDtypeStruct((B, S, D_DIFF), BF16),
        scratch_shapes=[pltpu.VMEM((S // tq, LANES, tq), BF16),
                        pltpu.VMEM((DIFF_V_DIM, 2 * tq), F32)],
        compiler_params=pltpu.CompilerParams(
            dimension_semantics=("parallel", "parallel"), vmem_limit_bytes=VMEM_LIMIT),
        name="diff_attention",
    )(lq1.reshape(1, -1), lk1.reshape(1, -1), lq2.reshape(1, -1), lk2.reshape(1, -1),
      subln_g.reshape(1, -1), z3, z3, z3, z3)


def _pool_kernel(u_ref, halo_ref, gp_ref, w_ref, ps_ref, o_ref, *, tm, per_seq):
    i = pl.program_id(0)
    g = pl.program_id(1)
    window = jnp.left_shift(jnp.int32(POOL_WINDOWS[0]), g)
    it = i % per_seq

    lag = (lax.broadcasted_iota(jnp.int32, (tm, tm), 0)
           - lax.broadcasted_iota(jnp.int32, (tm, tm), 1))
    band = jnp.logical_and(lag >= 0, lag < window).astype(BF16)
    u = u_ref[...]
    win = jnp.dot(band, u, preferred_element_type=F32)

    hlag = (lax.broadcasted_iota(jnp.int32, (POOL_HALO, POOL_HALO), 0) + POOL_HALO
            - lax.broadcasted_iota(jnp.int32, (POOL_HALO, POOL_HALO), 1))
    hband = (hlag < window).astype(BF16)
    hwin = jnp.dot(hband, halo_ref[...], preferred_element_type=F32)
    hwin = jnp.where(it == 0, 0.0, hwin)
    win = jnp.concatenate([win[:POOL_HALO] + hwin, win[POOL_HALO:]], axis=0)

    t1 = it * tm + lax.broadcasted_iota(jnp.int32, (tm, 1), 0) + 1
    count = jnp.minimum(t1, window).astype(F32)
    pooled = win / count - u.astype(F32)
    mixed = jnp.dot(pooled.astype(BF16), w_ref[0], preferred_element_type=F32) * ps_ref[...]
    o_ref[...] = (mixed * _silu(gp_ref[...].astype(F32))).astype(BF16)


def _pool_mixer(z2, w_pool_bf, pool_scale, *, seq):
    M = z2.shape[0]
    tm = 512
    per_seq = seq // tm
    gd = POOL_GROUP_DIM
    kern = functools.partial(_pool_kernel, tm=tm, per_seq=per_seq)
    return pl.pallas_call(
        kern,
        grid=(M // tm, N_POOL_GROUPS),
        in_specs=[
            pl.BlockSpec((tm, gd), lambda i, g: (i, g)),
            pl.BlockSpec((POOL_HALO, gd),
                         lambda i, g: (jnp.maximum(i * (tm // POOL_HALO) - 1, 0), g)),
            pl.BlockSpec((tm, gd), lambda i, g: (i, N_POOL_GROUPS + g)),
            pl.BlockSpec((1, gd, gd), lambda i, g: (g, 0, 0)),
            pl.BlockSpec((1, gd), lambda i, g: (0, g)),
        ],
        out_specs=pl.BlockSpec((tm, gd), lambda i, g: (i, g)),
        out_shape=jax.ShapeDtypeStruct((M, D_POOL), BF16),
        compiler_params=pltpu.CompilerParams(
            dimension_semantics=("parallel", "parallel"), vmem_limit_bytes=VMEM_LIMIT),
        name="pool_mixer",
    )(z2, z2, z2, w_pool_bf, pool_scale.reshape(1, D_POOL))


def _out_proj_kernel(p_ref, d_ref, w_ref, x_ref, gate_ref, gpost_ref, o_ref):
    y = (jnp.dot(p_ref[...], w_ref[:D_POOL, :], preferred_element_type=F32)
         + jnp.dot(d_ref[...], w_ref[D_POOL:, :], preferred_element_type=F32))
    ms = jnp.mean(y * y, axis=-1, keepdims=True)
    yn = y * lax.rsqrt(ms + EPS) * gpost_ref[...]
    o_ref[...] = x_ref[...] + (1.0 + gate_ref[0]) * yn


def _out_proj(pool_out, diff_out, w_out_bf, x2, mod3, g_post, *, seq):
    M, D = x2.shape
    tm = 256
    per_seq = seq // tm
    return pl.pallas_call(
        _out_proj_kernel,
        grid=(M // tm,),
        in_specs=[
            pl.BlockSpec((tm, D_POOL), lambda i: (i, 0)),
            pl.BlockSpec((tm, D_DIFF), lambda i: (i, 0)),
            pl.BlockSpec((D, D), lambda i: (0, 0)),
            pl.BlockSpec((tm, D), lambda i: (i, 0)),
            pl.BlockSpec((1, 1, D), lambda i: (i // per_seq, 0, 2)),
            pl.BlockSpec((1, D), lambda i: (0, 0)),
        ],
        out_specs=pl.BlockSpec((tm, D), lambda i: (i, 0)),
        out_shape=jax.ShapeDtypeStruct((M, D), F32),
        compiler_params=pltpu.CompilerParams(
            dimension_semantics=("parallel",), vmem_limit_bytes=VMEM_LIMIT),
        name="out_proj",
    )(pool_out, diff_out, w_out_bf, x2, mod3, g_post.reshape(1, D))


def kernel(x, c, positions, w_ada, b_ada, g_pre, w_in, w_pool, pool_scale, lambda_q1, lambda_k1,
           lambda_q2, lambda_k2, subln_g, w_out, g_post):
    B, S, D = x.shape
    depth = w_in.shape[0]
    cos, sin = _rope_tables(positions)
    mod = _ada_modulation(c, w_ada, b_ada)
    w_in_bf = w_in.astype(BF16)
    w_pool_bf = w_pool.astype(BF16)
    w_out_bf = w_out.astype(BF16)
    x2 = x.reshape(B * S, D)
    for l in range(depth):
        lam_init = 0.8 - 0.6 * math.exp(-0.3 * l)
        mod3 = mod[l].reshape(B, 1, 3 * D)
        z2 = _in_proj(x2, mod3, g_pre[l], w_in_bf[l], cos, sin, seq=S)
        diff_out = _diff_attention(z2.reshape(B, S, D_IN), lambda_q1[l], lambda_k1[l],
                                   lambda_q2[l], lambda_k2[l], subln_g[l], lam_init=lam_init)
        pool_out = _pool_mixer(z2, w_pool_bf[l], pool_scale[l], seq=S)
        x2 = _out_proj(pool_out, diff_out.reshape(B * S, D_DIFF), w_out_bf[l], x2, mod3,
                       g_post[l], seq=S)
    return x2.reshape(B, S, D)
```

```python
import functools
import math

import jax
import jax.numpy as jnp
from jax import lax
from jax.experimental import pallas as pl
from jax.experimental.pallas import tpu as pltpu

D_MODEL = 2048
CHUNK = 64
D_POOL = 1024
D_DIFF = 1024
POOL_WINDOWS = (2, 4, 8, 16)
N_POOL_GROUPS = 4
POOL_GROUP_DIM = 256
DIFF_HEADS = 8
DIFF_V_DIM = 128
DIFF_QK_DIM = 64
D_QK = 1024
D_IN = 6144
ROPE_THETA = 10000.0
EPS = 1e-6
NEG_INF = -1e30

LANES = 128
ONES_ROWS = 16
ATTN_TQ = 512
POOL_HALO = 16
VMEM_LIMIT = 56 * 1024 * 1024
LOG2_E = math.log2(math.e)

F32 = jnp.float32
BF16 = jnp.bfloat16


def _silu(g):
    return g * jax.nn.sigmoid(g)


def _rope_table_kernel(pos_ref, freq_ref, sign_ref, cos_ref, sin_ref):
    ang = pos_ref[0].astype(F32) * freq_ref[...]
    cos_ref[0] = jnp.cos(ang)
    sin_ref[0] = jnp.sin(ang) * sign_ref[...]


def _rope_tables(positions):
    B, S = positions.shape
    ts = 512
    inv_freq = 1.0 / (ROPE_THETA ** (jnp.arange(0, DIFF_QK_DIM, 2, dtype=F32) / DIFF_QK_DIM))
    freq = jnp.tile(inv_freq, LANES // (DIFF_QK_DIM // 2))[None, :]
    half = DIFF_QK_DIM // 2
    sign = jnp.where((jnp.arange(LANES) % DIFF_QK_DIM) < half, -1.0, 1.0).astype(F32)[None, :]
    cos, sin = pl.pallas_call(
        _rope_table_kernel,
        grid=(B, S // ts),
        in_specs=[
            pl.BlockSpec((1, ts, 1), lambda b, i: (b, i, 0)),
            pl.BlockSpec((1, LANES), lambda b, i: (0, 0)),
            pl.BlockSpec((1, LANES), lambda b, i: (0, 0)),
        ],
        out_specs=[
            pl.BlockSpec((1, ts, LANES), lambda b, i: (b, i, 0)),
            pl.BlockSpec((1, ts, LANES), lambda b, i: (b, i, 0)),
        ],
        out_shape=[jax.ShapeDtypeStruct((B, S, LANES), F32)] * 2,
        name="rope_tables",
    )(positions.reshape(B, S, 1), freq, sign)
    return cos.reshape(B * S, LANES), sin.reshape(B * S, LANES)


def _ada_kernel(c_ref, w_ref, b_ref, o_ref):
    o_ref[0] = jnp.dot(c_ref[...], w_ref[0], preferred_element_type=F32) + b_ref[0]


def _ada_modulation(c, w_ada, b_ada):
    depth, d, n = w_ada.shape
    B = c.shape[0]
    rows = 8
    c_pad = jnp.zeros((rows, d), F32).at[:B].set(c)
    tn = 1024
    out = pl.pallas_call(
        _ada_kernel,
        grid=(depth, n // tn),
        in_specs=[
            pl.BlockSpec((rows, d), lambda l, j: (0, 0)),
            pl.BlockSpec((1, d, tn), lambda l, j: (l, 0, j)),
            pl.BlockSpec((1, 1, tn), lambda l, j: (l, 0, j)),
        ],
        out_specs=pl.BlockSpec((1, rows, tn), lambda l, j: (l, 0, j)),
        out_shape=jax.ShapeDtypeStruct((depth, rows, n), F32),
        compiler_params=pltpu.CompilerParams(vmem_limit_bytes=VMEM_LIMIT),
        name="ada_modulation",
    )(c_pad, w_ada, b_ada.reshape(depth, 1, n))
    return out[:, :B]


def _in_proj_kernel(x_ref, shift_ref, scale_ref, gpre_ref, w_ref, cos_ref, sin_ref,
                    z_ref, h_ref, *, tm, tn):
    j = pl.program_id(1)

    @pl.when(j == 0)
    def _():
        x = x_ref[...]
        ms = jnp.mean(x * x, axis=-1, keepdims=True)
        y = x * lax.rsqrt(ms + EPS) * gpre_ref[...]
        h_ref[...] = (y * (1.0 + scale_ref[0]) + shift_ref[0]).astype(BF16)

    acc = jnp.dot(h_ref[...], w_ref[...], preferred_element_type=F32)
    q_tile = (2 * D_POOL) // tn
    k_tile = (2 * D_POOL + D_QK) // tn
    is_rope = jnp.logical_or(j == q_tile, j == k_tile)

    @pl.when(is_rope)
    def _():
        qk_scale = jnp.where(j == q_tile, LOG2_E / math.sqrt(DIFF_QK_DIM), 1.0).astype(F32)
        cos = cos_ref[...] * qk_scale
        sin = sin_ref[...] * qk_scale
        lane = lax.broadcasted_iota(jnp.int32, (tm, LANES), 1)
        lower = (lane % DIFF_QK_DIM) < (DIFF_QK_DIM // 2)
        for g in range(tn // LANES):
            xg = acc[:, g * LANES:(g + 1) * LANES]
            rot = jnp.where(lower,
                            pltpu.roll(xg, LANES - DIFF_QK_DIM // 2, 1),
                            pltpu.roll(xg, DIFF_QK_DIM // 2, 1))
            z_ref[:, g * LANES:(g + 1) * LANES] = (xg * cos + rot * sin).astype(BF16)

    @pl.when(jnp.logical_not(is_rope))
    def _():
        z_ref[...] = acc.astype(BF16)


def _in_proj(x2, mod3, g_pre, w_in_bf, cos, sin, *, seq):
    M, D = x2.shape
    N = w_in_bf.shape[1]
    tm, tn = 1024, 1024
    per_seq = seq // tm
    kern = functools.partial(_in_proj_kernel, tm=tm, tn=tn)
    return pl.pallas_call(
        kern,
        grid=(M // tm, N // tn),
        in_specs=[
            pl.BlockSpec((tm, D), lambda i, j: (i, 0)),
            pl.BlockSpec((1, 1, D), lambda i, j: (i // per_seq, 0, 0)),
            pl.BlockSpec((1, 1, D), lambda i, j: (i // per_seq, 0, 1)),
            pl.BlockSpec((1, D), lambda i, j: (0, 0)),
            pl.BlockSpec((D, tn), lambda i, j: (0, j)),
            pl.BlockSpec((tm, LANES), lambda i, j: (i, 0)),
            pl.BlockSpec((tm, LANES), lambda i, j: (i, 0)),
        ],
        out_specs=pl.BlockSpec((tm, tn), lambda i, j: (i, j)),
        out_shape=jax.ShapeDtypeStruct((M, N), BF16),
        scratch_shapes=[pltpu.VMEM((tm, D), BF16)],
        compiler_params=pltpu.CompilerParams(
            dimension_semantics=("parallel", "arbitrary"), vmem_limit_bytes=VMEM_LIMIT),
        name="in_proj",
    )(x2, mod3, mod3, g_pre.reshape(1, D), w_in_bf, cos, sin)


def _attn_kernel(lq1_ref, lk1_ref, lq2_ref, lk2_ref, subg_ref, q_ref, k_ref, v_ref, g_ref,
                 o_ref, vt_ref, acc_ref, s_ref, *, seq, tq, lam_init):
    tk = tq
    lam = (jnp.exp(jnp.sum(lq1_ref[...] * lk1_ref[...], axis=-1, keepdims=True))
           - jnp.exp(jnp.sum(lq2_ref[...] * lk2_ref[...], axis=-1, keepdims=True))
           + lam_init)

    for c in range(seq // tk):
        vt_ref[c, :DIFF_V_DIM, :] = v_ref[c * tk:(c + 1) * tk, :].astype(F32).T.astype(BF16)
        vt_ref[c, DIFF_V_DIM:, :] = jnp.ones((ONES_ROWS, tk), BF16)

    lane = lax.broadcasted_iota(jnp.int32, (tq, LANES), 1)
    first_map = lane < DIFF_QK_DIM
    key_chunk = lax.broadcasted_iota(jnp.int32, (tk, 2 * tq), 0) // CHUNK
    q_chunk = (lax.broadcasted_iota(jnp.int32, (tk, 2 * tq), 1) % tq) // CHUNK
    diag_mask = key_chunk <= q_chunk
    contract_last = (((1,), (1,)), ((), ()))

    def q_block(i, carry):
        q0 = pl.multiple_of(i * tq, tq)
        q = q_ref[pl.ds(q0, tq), :]
        zero = jnp.zeros_like(q)
        qcat = jnp.concatenate([jnp.where(first_map, q, zero),
                                jnp.where(first_map, zero, q)], axis=0)
        acc_ref[...] = jnp.zeros_like(acc_ref)

        def scores(j, slot):
            k0 = pl.multiple_of(j * tk, tk)
            s_ref[slot] = lax.dot_general(k_ref[pl.ds(k0, tk), :], qcat, contract_last,
                                          preferred_element_type=F32)

        def softmax_pv(j, slot, m, masked=False):
            s = s_ref[slot]
            if masked:
                s = jnp.where(diag_mask, s, NEG_INF)
            m_new = jnp.maximum(m, jnp.max(s, axis=0, keepdims=True))
            alpha = jnp.exp2(m - m_new)
            p = jnp.exp2(s - m_new).astype(BF16)
            pv = jnp.dot(vt_ref[j], p, preferred_element_type=F32)
            acc_ref[...] = acc_ref[...] * alpha + pv
            return m_new

        def kv_pair(t, m):
            j = 2 * t
            scores(j + 1, 1)
            m = softmax_pv(j, 0, m)
            scores(j + 2, 0)
            return softmax_pv(j + 1, 1, m)

        scores(0, 0)
        m0 = jnp.full((1, 2 * tq), NEG_INF, F32)
        m = lax.fori_loop(0, i // 2, kv_pair, m0)

        @pl.when(i % 2 == 0)
        def _():
            softmax_pv(i, 0, m, masked=True)

        @pl.when(i % 2 == 1)
        def _():
            scores(i, 1)
            m_odd = softmax_pv(i - 1, 0, m)
            softmax_pv(i, 1, m_odd, masked=True)

        o = acc_ref[:DIFF_V_DIM, :] / acc_ref[DIFF_V_DIM:DIFF_V_DIM + 1, :]
        att = (o[:, :tq] - lam * o[:, tq:]).T
        ms = jnp.mean(att * att, axis=-1, keepdims=True)
        y = att * lax.rsqrt(ms + EPS) * subg_ref[...] * (1.0 - lam_init)
        g = g_ref[pl.ds(q0, tq), :].astype(F32)
        o_ref[pl.ds(q0, tq), :] = (y * _silu(g)).astype(BF16)
        return carry

    lax.fori_loop(0, seq // tq, q_block, 0)


def _diff_attention(z3, lq1, lk1, lq2, lk2, subln_g, *, lam_init):
    B, S, _ = z3.shape
    tq = ATTN_TQ
    q_col = (2 * D_POOL) // LANES
    k_col = q_col + D_QK // LANES
    v_col = k_col + D_QK // LANES
    g_col = v_col + D_DIFF // LANES
    vec = lambda n: pl.BlockSpec((1, n), lambda b, h: (0, 0))
    head = lambda col: pl.BlockSpec((None, S, LANES), lambda b, h: (b, 0, col + h))
    kern = functools.partial(_attn_kernel, seq=S, tq=tq, lam_init=lam_init)
    return pl.pallas_call(
        kern,
        grid=(B, DIFF_HEADS),
        in_specs=[vec(DIFF_QK_DIM)] * 4 + [vec(DIFF_V_DIM),
                  head(q_col), head(k_col), head(v_col), head(g_col)],
        out_specs=pl.BlockSpec((None, S, LANES), lambda b, h: (b, 0, h)),
        out_shape=jax.ShapeDtypeStruct((B, S, D_DIFF), BF16),
        scratch_shapes=[pltpu.VMEM((S // tq, DIFF_V_DIM + ONES_ROWS, tq), BF16),
                        pltpu.VMEM((DIFF_V_DIM + ONES_ROWS, 2 * tq), F32),
                        pltpu.VMEM((2, tq, 2 * tq), F32)],
        compiler_params=pltpu.CompilerParams(
            dimension_semantics=("parallel", "parallel"), vmem_limit_bytes=VMEM_LIMIT),
        name="diff_attention",
    )(lq1.reshape(1, -1), lk1.reshape(1, -1), lq2.reshape(1, -1), lk2.reshape(1, -1),
      subln_g.reshape(1, -1), z3, z3, z3, z3)


def _pool_kernel(u_ref, halo_ref, gp_ref, w_ref, ps_ref, o_ref, *, tm, per_seq):
    i = pl.program_id(0)
    g = pl.program_id(1)
    window = jnp.left_shift(jnp.int32(POOL_WINDOWS[0]), g)
    it = i % per_seq

    lag = (lax.broadcasted_iota(jnp.int32, (tm, tm), 0)
           - lax.broadcasted_iota(jnp.int32, (tm, tm), 1))
    band = jnp.logical_and(lag >= 0, lag < window).astype(BF16)
    u = u_ref[...]
    win = jnp.dot(band, u, preferred_element_type=F32)

    hlag = (lax.broadcasted_iota(jnp.int32, (POOL_HALO, POOL_HALO), 0) + POOL_HALO
            - lax.broadcasted_iota(jnp.int32, (POOL_HALO, POOL_HALO), 1))
    hband = (hlag < window).astype(BF16)
    hwin = jnp.dot(hband, halo_ref[...], preferred_element_type=F32)
    hwin = jnp.where(it == 0, 0.0, hwin)
    win = jnp.concatenate([win[:POOL_HALO] + hwin, win[POOL_HALO:]], axis=0)

    t1 = it * tm + lax.broadcasted_iota(jnp.int32, (tm, 1), 0) + 1
    count = jnp.minimum(t1, window).astype(F32)
    pooled = win / count - u.astype(F32)
    mixed = jnp.dot(pooled.astype(BF16), w_ref[0], preferred_element_type=F32) * ps_ref[...]
    o_ref[...] = (mixed * _silu(gp_ref[...].astype(F32))).astype(BF16)


def _pool_mixer(z2, w_pool_bf, pool_scale, *, seq):
    M = z2.shape[0]
    tm = 512
    per_seq = seq // tm
    gd = POOL_GROUP_DIM
    kern = functools.partial(_pool_kernel, tm=tm, per_seq=per_seq)
    return pl.pallas_call(
        kern,
        grid=(M // tm, N_POOL_GROUPS),
        in_specs=[
            pl.BlockSpec((tm, gd), lambda i, g: (i, g)),
            pl.BlockSpec((POOL_HALO, gd),
                         lambda i, g: (jnp.maximum(i * (tm // POOL_HALO) - 1, 0), g)),
            pl.BlockSpec((tm, gd), lambda i, g: (i, N_POOL_GROUPS + g)),
            pl.BlockSpec((1, gd, gd), lambda i, g: (g, 0, 0)),
            pl.BlockSpec((1, gd), lambda i, g: (0, g)),
        ],
        out_specs=pl.BlockSpec((tm, gd), lambda i, g: (i, g)),
        out_shape=jax.ShapeDtypeStruct((M, D_POOL), BF16),
        compiler_params=pltpu.CompilerParams(
            dimension_semantics=("parallel", "parallel"), vmem_limit_bytes=VMEM_LIMIT),
        name="pool_mixer",
    )(z2, z2, z2, w_pool_bf, pool_scale.reshape(1, D_POOL))


def _out_proj_kernel(p_ref, d_ref, w_ref, x_ref, gate_ref, gpost_ref, o_ref):
    y = (jnp.dot(p_ref[...], w_ref[:D_POOL, :], preferred_element_type=F32)
         + jnp.dot(d_ref[...], w_ref[D_POOL:, :], preferred_element_type=F32))
    ms = jnp.mean(y * y, axis=-1, keepdims=True)
    yn = y * lax.rsqrt(ms + EPS) * gpost_ref[...]
    o_ref[...] = x_ref[...] + (1.0 + gate_ref[0]) * yn


def _out_proj(pool_out, diff_out, w_out_bf, x2, mod3, g_post, *, seq):
    M, D = x2.shape
    tm = 256
    per_seq = seq // tm
    return pl.pallas_call(
        _out_proj_kernel,
        grid=(M // tm,),
        in_specs=[
            pl.BlockSpec((tm, D_POOL), lambda i: (i, 0)),
            pl.BlockSpec((tm, D_DIFF), lambda i: (i, 0)),
            pl.BlockSpec((D, D), lambda i: (0, 0)),
            pl.BlockSpec((tm, D), lambda i: (i, 0)),
            pl.BlockSpec((1, 1, D), lambda i: (i // per_seq, 0, 2)),
            pl.BlockSpec((1, D), lambda i: (0, 0)),
        ],
        out_specs=pl.BlockSpec((tm, D), lambda i: (i, 0)),
        out_shape=jax.ShapeDtypeStruct((M, D), F32),
        compiler_params=pltpu.CompilerParams(
            dimension_semantics=("parallel",), vmem_limit_bytes=VMEM_LIMIT),
        name="out_proj",
    )(pool_out, diff_out, w_out_bf, x2, mod3, g_post.reshape(1, D))


def kernel(x, c, positions, w_ada, b_ada, g_pre, w_in, w_pool, pool_scale, lambda_q1, lambda_k1,
           lambda_q2, lambda_k2, subln_g, w_out, g_post):
    B, S, D = x.shape
    depth = w_in.shape[0]
    cos, sin = _rope_tables(positions)
    mod = _ada_modulation(c, w_ada, b_ada)
    w_in_bf = w_in.astype(BF16)
    w_pool_bf = w_pool.astype(BF16)
    w_out_bf = w_out.astype(BF16)
    x2 = x.reshape(B * S, D)
    for l in range(depth):
        lam_init = 0.8 - 0.6 * math.exp(-0.3 * l)
        mod3 = mod[l].reshape(B, 1, 3 * D)
        z2 = _in_proj(x2, mod3, g_pre[l], w_in_bf[l], cos, sin, seq=S)
        diff_out = _diff_attention(z2.reshape(B, S, D_IN), lambda_q1[l], lambda_k1[l],
                                   lambda_q2[l], lambda_k2[l], subln_g[l], lam_init=lam_init)
        pool_out = _pool_mixer(z2, w_pool_bf[l], pool_scale[l], seq=S)
        x2 = _out_proj(pool_out, diff_out.reshape(B * S, D_DIFF), w_out_bf[l], x2, mod3,
                       g_post[l], seq=S)
    return x2.reshape(B, S, D)
```

```python
import functools
import math

import jax
import jax.numpy as jnp
from jax import lax
from jax.experimental import pallas as pl
from jax.experimental.pallas import tpu as pltpu

D_MODEL = 2048
CHUNK = 64
D_POOL = 1024
D_DIFF = 1024
POOL_WINDOWS = (2, 4, 8, 16)
N_POOL_GROUPS = 4
POOL_GROUP_DIM = 256
DIFF_HEADS = 8
DIFF_V_DIM = 128
DIFF_QK_DIM = 64
D_QK = 1024
D_IN = 6144
ROPE_THETA = 10000.0
EPS = 1e-6
NEG_INF = -1e30

LANES = 128
ONES_ROWS = 16
ATTN_TQ = 512
POOL_HALO = 16
VMEM_LIMIT = 56 * 1024 * 1024
LOG2_E = math.log2(math.e)

F32 = jnp.float32
BF16 = jnp.bfloat16


def _silu(g):
    return g * jax.nn.sigmoid(g)


def _rope_table_kernel(pos_ref, freq_ref, sign_ref, cos_ref, sin_ref):
    ang = pos_ref[0].astype(F32) * freq_ref[...]
    cos_ref[0] = jnp.cos(ang)
    sin_ref[0] = jnp.sin(ang) * sign_ref[...]


def _rope_tables(positions):
    B, S = positions.shape
    ts = 512
    inv_freq = 1.0 / (ROPE_THETA ** (jnp.arange(0, DIFF_QK_DIM, 2, dtype=F32) / DIFF_QK_DIM))
    freq = jnp.tile(inv_freq, LANES // (DIFF_QK_DIM // 2))[None, :]
    sign = jnp.where(jnp.arange(LANES) < LANES // 2, -1.0, 1.0).astype(F32)[None, :]
    cos, sin = pl.pallas_call(
        _rope_table_kernel,
        grid=(B, S // ts),
        in_specs=[
            pl.BlockSpec((1, ts, 1), lambda b, i: (b, i, 0)),
            pl.BlockSpec((1, LANES), lambda b, i: (0, 0)),
            pl.BlockSpec((1, LANES), lambda b, i: (0, 0)),
        ],
        out_specs=[
            pl.BlockSpec((1, ts, LANES), lambda b, i: (b, i, 0)),
            pl.BlockSpec((1, ts, LANES), lambda b, i: (b, i, 0)),
        ],
        out_shape=[jax.ShapeDtypeStruct((B, S, LANES), F32)] * 2,
        name="rope_tables",
    )(positions.reshape(B, S, 1), freq, sign)
    return cos.reshape(B * S, LANES), sin.reshape(B * S, LANES)


def _ada_kernel(c_ref, w_ref, b_ref, o_ref):
    o_ref[0] = jnp.dot(c_ref[...], w_ref[0], preferred_element_type=F32) + b_ref[0]


def _ada_modulation(c, w_ada, b_ada):
    depth, d, n = w_ada.shape
    B = c.shape[0]
    rows = 8
    c_pad = jnp.zeros((rows, d), F32).at[:B].set(c)
    tn = 1024
    out = pl.pallas_call(
        _ada_kernel,
        grid=(depth, n // tn),
        in_specs=[
            pl.BlockSpec((rows, d), lambda l, j: (0, 0)),
            pl.BlockSpec((1, d, tn), lambda l, j: (l, 0, j)),
            pl.BlockSpec((1, 1, tn), lambda l, j: (l, 0, j)),
        ],
        out_specs=pl.BlockSpec((1, rows, tn), lambda l, j: (l, 0, j)),
        out_shape=jax.ShapeDtypeStruct((depth, rows, n), F32),
        compiler_params=pltpu.CompilerParams(vmem_limit_bytes=VMEM_LIMIT),
        name="ada_modulation",
    )(c_pad, w_ada, b_ada.reshape(depth, 1, n))
    return out[:, :B]


def _in_proj_kernel(x_ref, shift_ref, scale_ref, gpre_ref, w_ref, cos_ref, sin_ref,
                    z_ref, h_ref, *, tm, tn):
    j = pl.program_id(1)

    @pl.when(j == 0)
    def _():
        x = x_ref[...]
        ms = jnp.mean(x * x, axis=-1, keepdims=True)
        y = x * lax.rsqrt(ms + EPS) * gpre_ref[...]
        h_ref[...] = (y * (1.0 + scale_ref[0]) + shift_ref[0]).astype(BF16)

    acc = jnp.dot(h_ref[...], w_ref[...], preferred_element_type=F32)
    q_tile = (2 * D_POOL) // tn
    k_tile = (2 * D_POOL + D_QK) // tn
    is_rope = jnp.logical_or(j == q_tile, j == k_tile)

    @pl.when(is_rope)
    def _():
        qk_scale = jnp.where(j == q_tile, LOG2_E / math.sqrt(DIFF_QK_DIM), 1.0).astype(F32)
        cos = cos_ref[...] * qk_scale
        sin = sin_ref[...] * qk_scale
        for g in range(tn // LANES):
            xg = acc[:, g * LANES:(g + 1) * LANES]
            rot = pltpu.roll(xg, LANES // 2, 1)
            z_ref[:, g * LANES:(g + 1) * LANES] = (xg * cos + rot * sin).astype(BF16)

    @pl.when(jnp.logical_not(is_rope))
    def _():
        z_ref[...] = acc.astype(BF16)


def _in_proj(x2, mod3, g_pre, w_in_bf, cos, sin, *, seq):
    M, D = x2.shape
    N = w_in_bf.shape[1]
    tm, tn = 1024, 1024
    per_seq = seq // tm
    kern = functools.partial(_in_proj_kernel, tm=tm, tn=tn)
    return pl.pallas_call(
        kern,
        grid=(M // tm, N // tn),
        in_specs=[
            pl.BlockSpec((tm, D), lambda i, j: (i, 0)),
            pl.BlockSpec((1, 1, D), lambda i, j: (i // per_seq, 0, 0)),
            pl.BlockSpec((1, 1, D), lambda i, j: (i // per_seq, 0, 1)),
            pl.BlockSpec((1, D), lambda i, j: (0, 0)),
            pl.BlockSpec((D, tn), lambda i, j: (0, j)),
            pl.BlockSpec((tm, LANES), lambda i, j: (i, 0)),
            pl.BlockSpec((tm, LANES), lambda i, j: (i, 0)),
        ],
        out_specs=pl.BlockSpec((tm, tn), lambda i, j: (i, j)),
        out_shape=jax.ShapeDtypeStruct((M, N), BF16),
        scratch_shapes=[pltpu.VMEM((tm, D), BF16)],
        compiler_params=pltpu.CompilerParams(
            dimension_semantics=("parallel", "arbitrary"), vmem_limit_bytes=VMEM_LIMIT),
        name="in_proj",
    )(x2, mod3, mod3, g_pre.reshape(1, D), w_in_bf, cos, sin)


def _attn_kernel(lq1_ref, lk1_ref, lq2_ref, lk2_ref, subg_ref, q_ref, k_ref, v_ref, g_ref,
                 o_ref, vt_ref, acc_ref, s_ref, *, seq, tq, lam_init):
    tk = tq
    lam = (jnp.exp(jnp.sum(lq1_ref[...] * lk1_ref[...], axis=-1, keepdims=True))
           - jnp.exp(jnp.sum(lq2_ref[...] * lk2_ref[...], axis=-1, keepdims=True))
           + lam_init)

    for c in range(seq // tk):
        vt_ref[c, :DIFF_V_DIM, :] = v_ref[c * tk:(c + 1) * tk, :].astype(F32).T.astype(BF16)
        vt_ref[c, DIFF_V_DIM:, :] = jnp.ones((ONES_ROWS, tk), BF16)

    lane = lax.broadcasted_iota(jnp.int32, (tq, LANES), 1)
    first_map = (lane // (DIFF_QK_DIM // 2)) % 2 == 0
    key_chunk = lax.broadcasted_iota(jnp.int32, (tk, 2 * tq), 0) // CHUNK
    q_chunk = (lax.broadcasted_iota(jnp.int32, (tk, 2 * tq), 1) % tq) // CHUNK
    diag_mask = key_chunk <= q_chunk
    contract_last = (((1,), (1,)), ((), ()))

    def q_block(i, carry):
        q0 = pl.multiple_of(i * tq, tq)
        q = q_ref[pl.ds(q0, tq), :]
        zero = jnp.zeros_like(q)
        qcat = jnp.concatenate([jnp.where(first_map, q, zero),
                                jnp.where(first_map, zero, q)], axis=0)
        acc_ref[...] = jnp.zeros_like(acc_ref)

        def scores(j, slot):
            k0 = pl.multiple_of(j * tk, tk)
            s_ref[slot] = lax.dot_general(k_ref[pl.ds(k0, tk), :], qcat, contract_last,
                                          preferred_element_type=F32)

        def softmax_pv(j, slot, m, masked=False):
            s = s_ref[slot]
            if masked:
                s = jnp.where(diag_mask, s, NEG_INF)
            m_new = jnp.maximum(m, jnp.max(s, axis=0, keepdims=True))
            alpha = jnp.exp2(m - m_new)
            p = jnp.exp2(s - m_new).astype(BF16)
            pv = jnp.dot(vt_ref[j], p, preferred_element_type=F32)
            acc_ref[...] = acc_ref[...] * alpha + pv
            return m_new

        def kv_pair(t, m):
            j = 2 * t
            scores(j + 1, 1)
            m = softmax_pv(j, 0, m)
            scores(j + 2, 0)
            return softmax_pv(j + 1, 1, m)

        scores(0, 0)
        m0 = jnp.full((1, 2 * tq), NEG_INF, F32)
        m = lax.fori_loop(0, i // 2, kv_pair, m0)

        @pl.when(i % 2 == 0)
        def _():
            softmax_pv(i, 0, m, masked=True)

        @pl.when(i % 2 == 1)
        def _():
            scores(i, 1)
            m_odd = softmax_pv(i - 1, 0, m)
            softmax_pv(i, 1, m_odd, masked=True)

        o = acc_ref[:DIFF_V_DIM, :] / acc_ref[DIFF_V_DIM:DIFF_V_DIM + 1, :]
        att = (o[:, :tq] - lam * o[:, tq:]).T
        ms = jnp.mean(att * att, axis=-1, keepdims=True)
        y = att * lax.rsqrt(ms + EPS) * subg_ref[...] * (1.0 - lam_init)
        g = g_ref[pl.ds(q0, tq), :].astype(F32)
        o_ref[pl.ds(q0, tq), :] = (y * _silu(g)).astype(BF16)
        return carry

    lax.fori_loop(0, seq // tq, q_block, 0)


def _diff_attention(z3, lq1, lk1, lq2, lk2, subln_g, *, lam_init):
    B, S, _ = z3.shape
    tq = ATTN_TQ
    q_col = (2 * D_POOL) // LANES
    k_col = q_col + D_QK // LANES
    v_col = k_col + D_QK // LANES
    g_col = v_col + D_DIFF // LANES
    vec = lambda n: pl.BlockSpec((1, n), lambda b, h: (0, 0))
    head = lambda col: pl.BlockSpec((None, S, LANES), lambda b, h: (b, 0, col + h))
    kern = functools.partial(_attn_kernel, seq=S, tq=tq, lam_init=lam_init)
    return pl.pallas_call(
        kern,
        grid=(B, DIFF_HEADS),
        in_specs=[vec(DIFF_QK_DIM)] * 4 + [vec(DIFF_V_DIM),
                  head(q_col), head(k_col), head(v_col), head(g_col)],
        out_specs=pl.BlockSpec((None, S, LANES), lambda b, h: (b, 0, h)),
        out_shape=jax.ShapeDtypeStruct((B, S, D_DIFF), BF16),
        scratch_shapes=[pltpu.VMEM((S // tq, DIFF_V_DIM + ONES_ROWS, tq), BF16),
                        pltpu.VMEM((DIFF_V_DIM + ONES_ROWS, 2 * tq), F32),
                        pltpu.VMEM((2, tq, 2 * tq), F32)],
        compiler_params=pltpu.CompilerParams(
            dimension_semantics=("parallel", "parallel"), vmem_limit_bytes=VMEM_LIMIT),
        name="diff_attention",
    )(lq1.reshape(1, -1), lk1.reshape(1, -1), lq2.reshape(1, -1), lk2.reshape(1, -1),
      subln_g.reshape(1, -1), z3, z3, z3, z3)


def _pool_bands(tm):
    lag = jnp.arange(tm)[:, None] - jnp.arange(tm)[None, :]
    hlag = jnp.arange(POOL_HALO)[:, None] + POOL_HALO - jnp.arange(POOL_HALO)[None, :]
    band = jnp.stack([(lag >= 0) & (lag < w) for w in POOL_WINDOWS]).astype(BF16)
    hband = jnp.stack([hlag < w for w in POOL_WINDOWS]).astype(BF16)
    return band, hband


def _out_proj_kernel(u_ref, halo_ref, gp_ref, d_ref, band_ref, hband_ref, wp_ref, ps_ref,
                     w_ref, x_ref, gate_ref, gpost_ref, o_ref, pool_ref, *, tm, per_seq):
    it = pl.program_id(0) % per_seq
    t1 = it * tm + lax.broadcasted_iota(jnp.int32, (tm, 1), 0) + 1
    y = jnp.dot(d_ref[...], w_ref[D_POOL:, :], preferred_element_type=F32)

    gd = POOL_GROUP_DIM
    for g, window in enumerate(POOL_WINDOWS):
        cols = slice(g * gd, (g + 1) * gd)
        u = u_ref[:, cols]
        win = jnp.dot(band_ref[g], u, preferred_element_type=F32)
        hwin = jnp.dot(hband_ref[g], halo_ref[:, cols], preferred_element_type=F32)
        hwin = jnp.where(it == 0, 0.0, hwin)
        win = jnp.concatenate([win[:POOL_HALO] + hwin, win[POOL_HALO:]], axis=0)
        count = jnp.minimum(t1, window).astype(F32)
        pooled = win / count - u.astype(F32)
        mixed = jnp.dot(pooled.astype(BF16), wp_ref[g], preferred_element_type=F32) * ps_ref[:, cols]
        pool_ref[:, cols] = (mixed * _silu(gp_ref[:, cols].astype(F32))).astype(BF16)

    y = y + jnp.dot(pool_ref[...], w_ref[:D_POOL, :], preferred_element_type=F32)
    ms = jnp.mean(y * y, axis=-1, keepdims=True)
    yn = y * lax.rsqrt(ms + EPS) * gpost_ref[...]
    o_ref[...] = x_ref[...] + (1.0 + gate_ref[0]) * yn


def _out_proj(z2, diff_out, w_pool_bf, pool_scale, w_out_bf, x2, mod3, g_post, *, seq):
    M, D = x2.shape
    tm = 512
    per_seq = seq // tm
    band, hband = _pool_bands(tm)
    const = lambda shape: pl.BlockSpec(shape, lambda i: (0,) * len(shape))
    kern = functools.partial(_out_proj_kernel, tm=tm, per_seq=per_seq)
    return pl.pallas_call(
        kern,
        grid=(M // tm,),
        in_specs=[
            pl.BlockSpec((tm, D_POOL), lambda i: (i, 0)),
            pl.BlockSpec((POOL_HALO, D_POOL),
                         lambda i: (jnp.maximum(i * (tm // POOL_HALO) - 1, 0), 0)),
            pl.BlockSpec((tm, D_POOL), lambda i: (i, 1)),
            pl.BlockSpec((tm, D_DIFF), lambda i: (i, 0)),
            const(band.shape), const(hband.shape), const(w_pool_bf.shape), const((1, D_POOL)),
            const((D, D)),
            pl.BlockSpec((tm, D), lambda i: (i, 0)),
            pl.BlockSpec((1, 1, D), lambda i: (i // per_seq, 0, 2)),
            const((1, D)),
        ],
        out_specs=pl.BlockSpec((tm, D), lambda i: (i, 0)),
        out_shape=jax.ShapeDtypeStruct((M, D), F32),
        scratch_shapes=[pltpu.VMEM((tm, D_POOL), BF16)],
        compiler_params=pltpu.CompilerParams(
            dimension_semantics=("parallel",), vmem_limit_bytes=VMEM_LIMIT),
        name="out_proj",
    )(z2, z2, z2, diff_out, band, hband, w_pool_bf, pool_scale.reshape(1, D_POOL),
      w_out_bf, x2, mod3, g_post.reshape(1, D))


def _permute_qk_columns(w_in):
    depth, d, _ = w_in.shape
    lo, hi = 2 * D_POOL, 2 * D_POOL + 2 * D_QK
    half = DIFF_QK_DIM // 2
    qk = w_in[:, :, lo:hi].reshape(depth, d, 2 * DIFF_HEADS, 2, 2, half)
    qk = qk.transpose(0, 1, 2, 4, 3, 5).reshape(depth, d, 2 * D_QK)
    return jnp.concatenate([w_in[:, :, :lo], qk, w_in[:, :, hi:]], axis=-1)


def kernel(x, c, positions, w_ada, b_ada, g_pre, w_in, w_pool, pool_scale, lambda_q1, lambda_k1,
           lambda_q2, lambda_k2, subln_g, w_out, g_post):
    B, S, D = x.shape
    depth = w_in.shape[0]
    cos, sin = _rope_tables(positions)
    mod = _ada_modulation(c, w_ada, b_ada)
    w_in_bf = _permute_qk_columns(w_in).astype(BF16)
    w_pool_bf = w_pool.astype(BF16)
    w_out_bf = w_out.astype(BF16)
    x2 = x.reshape(B * S, D)
    for l in range(depth):
        lam_init = 0.8 - 0.6 * math.exp(-0.3 * l)
        mod3 = mod[l].reshape(B, 1, 3 * D)
        z2 = _in_proj(x2, mod3, g_pre[l], w_in_bf[l], cos, sin, seq=S)
        diff_out = _diff_attention(z2.reshape(B, S, D_IN), lambda_q1[l], lambda_k1[l],
                                   lambda_q2[l], lambda_k2[l], subln_g[l], lam_init=lam_init)
        x2 = _out_proj(z2, diff_out.reshape(B * S, D_DIFF), w_pool_bf[l], pool_scale[l],
                       w_out_bf[l], x2, mod3, g_post[l], seq=S)
    return x2.reshape(B, S, D)
```

```python
import functools
import math

import jax
import jax.numpy as jnp
from jax import lax
from jax.experimental import pallas as pl
from jax.experimental.pallas import tpu as pltpu

D_MODEL = 2048
CHUNK = 64
D_POOL = 1024
D_DIFF = 1024
POOL_WINDOWS = (2, 4, 8, 16)
N_POOL_GROUPS = 4
POOL_GROUP_DIM = 256
DIFF_HEADS = 8
DIFF_V_DIM = 128
DIFF_QK_DIM = 64
D_QK = 1024
D_IN = 6144
ROPE_THETA = 10000.0
EPS = 1e-6
NEG_INF = -1e30

LANES = 128
ONES_ROWS = 16
ATTN_TQ = 512
POOL_HALO = 16
VMEM_LIMIT = 56 * 1024 * 1024
LOG2_E = math.log2(math.e)

F32 = jnp.float32
BF16 = jnp.bfloat16


def _silu(g):
    return g * jax.nn.sigmoid(g)


def _rope_table_kernel(pos_ref, freq_ref, sign_ref, cos_ref, sin_ref):
    ang = pos_ref[0].astype(F32) * freq_ref[...]
    cos_ref[0] = jnp.cos(ang)
    sin_ref[0] = jnp.sin(ang) * sign_ref[...]


def _rope_tables(positions):
    B, S = positions.shape
    ts = 512
    inv_freq = 1.0 / (ROPE_THETA ** (jnp.arange(0, DIFF_QK_DIM, 2, dtype=F32) / DIFF_QK_DIM))
    freq = jnp.tile(inv_freq, LANES // (DIFF_QK_DIM // 2))[None, :]
    sign = jnp.where(jnp.arange(LANES) < LANES // 2, -1.0, 1.0).astype(F32)[None, :]
    cos, sin = pl.pallas_call(
        _rope_table_kernel,
        grid=(B, S // ts),
        in_specs=[
            pl.BlockSpec((1, ts, 1), lambda b, i: (b, i, 0)),
            pl.BlockSpec((1, LANES), lambda b, i: (0, 0)),
            pl.BlockSpec((1, LANES), lambda b, i: (0, 0)),
        ],
        out_specs=[
            pl.BlockSpec((1, ts, LANES), lambda b, i: (b, i, 0)),
            pl.BlockSpec((1, ts, LANES), lambda b, i: (b, i, 0)),
        ],
        out_shape=[jax.ShapeDtypeStruct((B, S, LANES), F32)] * 2,
        name="rope_tables",
    )(positions.reshape(B, S, 1), freq, sign)
    return cos.reshape(B * S, LANES), sin.reshape(B * S, LANES)


def _ada_kernel(c_ref, w_ref, b_ref, o_ref):
    o_ref[0] = jnp.dot(c_ref[...], w_ref[0], preferred_element_type=F32) + b_ref[0]


def _ada_modulation(c, w_ada, b_ada):
    depth, d, n = w_ada.shape
    B = c.shape[0]
    rows = 8
    c_pad = jnp.zeros((rows, d), F32).at[:B].set(c)
    tn = 1024
    out = pl.pallas_call(
        _ada_kernel,
        grid=(depth, n // tn),
        in_specs=[
            pl.BlockSpec((rows, d), lambda l, j: (0, 0)),
            pl.BlockSpec((1, d, tn), lambda l, j: (l, 0, j)),
            pl.BlockSpec((1, 1, tn), lambda l, j: (l, 0, j)),
        ],
        out_specs=pl.BlockSpec((1, rows, tn), lambda l, j: (l, 0, j)),
        out_shape=jax.ShapeDtypeStruct((depth, rows, n), F32),
        compiler_params=pltpu.CompilerParams(vmem_limit_bytes=VMEM_LIMIT),
        name="ada_modulation",
    )(c_pad, w_ada, b_ada.reshape(depth, 1, n))
    return out[:, :B]


def _in_proj_kernel(x_ref, shift_ref, scale_ref, gpre_ref, w_ref, cos_ref, sin_ref,
                    z_ref, h_ref, *, tm, tn):
    j = pl.program_id(1)

    @pl.when(j == 0)
    def _():
        x = x_ref[...]
        ms = jnp.mean(x * x, axis=-1, keepdims=True)
        y = x * lax.rsqrt(ms + EPS) * gpre_ref[...]
        h_ref[...] = (y * (1.0 + scale_ref[0]) + shift_ref[0]).astype(BF16)

    acc = jnp.dot(h_ref[...], w_ref[...], preferred_element_type=F32)
    q_tile = (2 * D_POOL) // tn
    k_tile = (2 * D_POOL + D_QK) // tn
    is_rope = jnp.logical_or(j == q_tile, j == k_tile)

    @pl.when(is_rope)
    def _():
        qk_scale = jnp.where(j == q_tile, LOG2_E / math.sqrt(DIFF_QK_DIM), 1.0).astype(F32)
        cos = cos_ref[...] * qk_scale
        sin = sin_ref[...] * qk_scale
        for g in range(tn // LANES):
            xg = acc[:, g * LANES:(g + 1) * LANES]
            rot = pltpu.roll(xg, LANES // 2, 1)
            z_ref[:, g * LANES:(g + 1) * LANES] = (xg * cos + rot * sin).astype(BF16)

    @pl.when(jnp.logical_not(is_rope))
    def _():
        z_ref[...] = acc.astype(BF16)


def _in_proj(x2, mod3, g_pre, w_in_bf, cos, sin, *, seq):
    M, D = x2.shape
    N = w_in_bf.shape[1]
    tm, tn = 1024, 1024
    per_seq = seq // tm
    kern = functools.partial(_in_proj_kernel, tm=tm, tn=tn)
    return pl.pallas_call(
        kern,
        grid=(M // tm, N // tn),
        in_specs=[
            pl.BlockSpec((tm, D), lambda i, j: (i, 0)),
            pl.BlockSpec((1, 1, D), lambda i, j: (i // per_seq, 0, 0)),
            pl.BlockSpec((1, 1, D), lambda i, j: (i // per_seq, 0, 1)),
            pl.BlockSpec((1, D), lambda i, j: (0, 0)),
            pl.BlockSpec((D, tn), lambda i, j: (0, j)),
            pl.BlockSpec((tm, LANES), lambda i, j: (i, 0)),
            pl.BlockSpec((tm, LANES), lambda i, j: (i, 0)),
        ],
        out_specs=pl.BlockSpec((tm, tn), lambda i, j: (i, j)),
        out_shape=jax.ShapeDtypeStruct((M, N), BF16),
        scratch_shapes=[pltpu.VMEM((tm, D), BF16)],
        compiler_params=pltpu.CompilerParams(
            dimension_semantics=("parallel", "arbitrary"), vmem_limit_bytes=VMEM_LIMIT),
        name="in_proj",
    )(x2, mod3, mod3, g_pre.reshape(1, D), w_in_bf, cos, sin)


def _attn_kernel(lq1_ref, lk1_ref, lq2_ref, lk2_ref, subg_ref, q_ref, k_ref, v_ref, g_ref,
                 o_ref, vt_ref, qcat_ref, acc_ref, s_ref, *, seq, tq, lam_init):
    tk, hq, nq = tq, tq // 2, seq // tq
    lam = (jnp.exp(jnp.sum(lq1_ref[...] * lk1_ref[...], axis=-1, keepdims=True))
           - jnp.exp(jnp.sum(lq2_ref[...] * lk2_ref[...], axis=-1, keepdims=True))
           + lam_init)

    lane = lax.broadcasted_iota(jnp.int32, (tq, LANES), 1)
    first_map = (lane // (DIFF_QK_DIM // 2)) % 2 == 0
    for c in range(nq):
        rows = slice(c * tq, (c + 1) * tq)
        vt_ref[c, :DIFF_V_DIM, :] = v_ref[rows, :].astype(F32).T.astype(BF16)
        vt_ref[c, DIFF_V_DIM:, :] = jnp.ones((ONES_ROWS, tk), BF16)
        q = q_ref[rows, :]
        zero = jnp.zeros_like(q)
        q0, q1 = jnp.where(first_map, q, zero), jnp.where(first_map, zero, q)
        qcat_ref[c] = jnp.concatenate([q0[:hq], q1[:hq], q0[hq:], q1[hq:]], axis=0)

    key_a = lax.broadcasted_iota(jnp.int32, (hq, 2 * tq), 0)
    col_a = lax.broadcasted_iota(jnp.int32, (hq, 2 * tq), 1)
    mask_a = key_a // CHUNK <= (col_a % hq + hq * (col_a // tq)) // CHUNK
    key_b = lax.broadcasted_iota(jnp.int32, (hq, tq), 0)
    col_b = lax.broadcasted_iota(jnp.int32, (hq, tq), 1)
    mask_b = key_b // CHUNK <= (col_b % hq) // CHUNK
    contract_last = (((1,), (1,)), ((), ()))

    def scores(tile, slot):
        kind, i, j = tile
        if kind == "full":
            s_ref[slot] = lax.dot_general(k_ref[j * tk:(j + 1) * tk, :], qcat_ref[i],
                                          contract_last, preferred_element_type=F32)
        elif kind == "diag_a":
            s_ref[slot, :hq, :] = lax.dot_general(k_ref[i * tk:i * tk + hq, :], qcat_ref[i],
                                                  contract_last, preferred_element_type=F32)
        else:
            s_ref[slot, :hq, :tq] = lax.dot_general(k_ref[i * tk + hq:(i + 1) * tk, :],
                                                    qcat_ref[i, tq:, :], contract_last,
                                                    preferred_element_type=F32)

    def softmax_pv(tile, slot, m):
        kind, i, j = tile
        acc = acc_ref.at[i % 2]
        if kind == "full":
            s, vt, cols = s_ref[slot], vt_ref[j], slice(None)
        elif kind == "diag_a":
            s = jnp.where(mask_a, s_ref[slot, :hq, :], NEG_INF)
            vt, cols = vt_ref[i, :, :hq], slice(None)
        else:
            s = jnp.where(mask_b, s_ref[slot, :hq, :tq], NEG_INF)
            vt, cols, m = vt_ref[i, :, hq:], slice(tq, None), m[:, tq:]
        m_new = jnp.max(s, axis=0, keepdims=True)
        if m is not None:
            m_new = jnp.maximum(m, m_new)
        p = jnp.exp2(s - m_new).astype(BF16)
        pv = jnp.dot(vt, p, preferred_element_type=F32)
        if m is None:
            acc[...] = pv
        else:
            acc[:, cols] = acc[:, cols] * jnp.exp2(m - m_new) + pv
        return m_new

    def finish(i):
        acc = acc_ref.at[i % 2]
        o = acc[:DIFF_V_DIM, :] / acc[DIFF_V_DIM:DIFF_V_DIM + 1, :]
        o0 = jnp.concatenate([o[:, :hq], o[:, tq:tq + hq]], axis=1)
        o1 = jnp.concatenate([o[:, hq:tq], o[:, tq + hq:]], axis=1)
        att = (o0 - lam * o1).T
        ms = jnp.mean(att * att, axis=-1, keepdims=True)
        y = att * lax.rsqrt(ms + EPS) * subg_ref[...] * (1.0 - lam_init)
        rows = slice(i * tq, (i + 1) * tq)
        o_ref[rows, :] = (y * _silu(g_ref[rows, :].astype(F32))).astype(BF16)

    tiles = []
    for i in range(nq):
        tiles += [("full", i, j) for j in range(i)] + [("diag_a", i, i), ("diag_b", i, i)]
    scores(tiles[0], 0)
    m = None
    for n, tile in enumerate(tiles):
        if n + 1 < len(tiles):
            scores(tiles[n + 1], (n + 1) % 2)
        m = softmax_pv(tile, n % 2, m)
        if tile[0] == "diag_b":
            finish(tile[1])
            m = None


def _diff_attention(z3, lq1, lk1, lq2, lk2, subln_g, *, lam_init):
    B, S, _ = z3.shape
    tq = ATTN_TQ
    q_col = (2 * D_POOL) // LANES
    k_col = q_col + D_QK // LANES
    v_col = k_col + D_QK // LANES
    g_col = v_col + D_DIFF // LANES
    vec = lambda n: pl.BlockSpec((1, n), lambda b, h: (0, 0))
    head = lambda col: pl.BlockSpec((None, S, LANES), lambda b, h: (b, 0, col + h))
    kern = functools.partial(_attn_kernel, seq=S, tq=tq, lam_init=lam_init)
    return pl.pallas_call(
        kern,
        grid=(B, DIFF_HEADS),
        in_specs=[vec(DIFF_QK_DIM)] * 4 + [vec(DIFF_V_DIM),
                  head(q_col), head(k_col), head(v_col), head(g_col)],
        out_specs=pl.BlockSpec((None, S, LANES), lambda b, h: (b, 0, h)),
        out_shape=jax.ShapeDtypeStruct((B, S, D_DIFF), BF16),
        scratch_shapes=[pltpu.VMEM((S // tq, DIFF_V_DIM + ONES_ROWS, tq), BF16),
                        pltpu.VMEM((S // tq, 2 * tq, LANES), BF16),
                        pltpu.VMEM((2, DIFF_V_DIM + ONES_ROWS, 2 * tq), F32),
                        pltpu.VMEM((2, tq, 2 * tq), F32)],
        compiler_params=pltpu.CompilerParams(
            dimension_semantics=("parallel", "parallel"), vmem_limit_bytes=VMEM_LIMIT),
        name="diff_attention",
    )(lq1.reshape(1, -1), lk1.reshape(1, -1), lq2.reshape(1, -1), lk2.reshape(1, -1),
      subln_g.reshape(1, -1), z3, z3, z3, z3)


def _pool_bands(tm):
    lag = jnp.arange(tm)[:, None] - jnp.arange(tm)[None, :]
    hlag = jnp.arange(POOL_HALO)[:, None] + POOL_HALO - jnp.arange(POOL_HALO)[None, :]
    band = jnp.stack([(lag >= 0) & (lag < w) for w in POOL_WINDOWS]).astype(BF16)
    hband = jnp.stack([hlag < w for w in POOL_WINDOWS]).astype(BF16)
    return band, hband


def _out_proj_kernel(u_ref, halo_ref, gp_ref, d_ref, band_ref, hband_ref, wp_ref, ps_ref,
                     w_ref, x_ref, gate_ref, gpost_ref, o_ref, pool_ref, *, tm, per_seq):
    it = pl.program_id(0) % per_seq
    t1 = it * tm + lax.broadcasted_iota(jnp.int32, (tm, 1), 0) + 1
    y = jnp.dot(d_ref[...], w_ref[D_POOL:, :], preferred_element_type=F32)

    gd = POOL_GROUP_DIM
    for g, window in enumerate(POOL_WINDOWS):
        cols = slice(g * gd, (g + 1) * gd)
        u = u_ref[:, cols]
        win = jnp.dot(band_ref[g], u, preferred_element_type=F32)
        hwin = jnp.dot(hband_ref[g], halo_ref[:, cols], preferred_element_type=F32)
        hwin = jnp.where(it == 0, 0.0, hwin)
        win = jnp.concatenate([win[:POOL_HALO] + hwin, win[POOL_HALO:]], axis=0)
        count = jnp.minimum(t1, window).astype(F32)
        pooled = win / count - u.astype(F32)
        mixed = jnp.dot(pooled.astype(BF16), wp_ref[g], preferred_element_type=F32) * ps_ref[:, cols]
        pool_ref[:, cols] = (mixed * _silu(gp_ref[:, cols].astype(F32))).astype(BF16)

    y = y + jnp.dot(pool_ref[...], w_ref[:D_POOL, :], preferred_element_type=F32)
    ms = jnp.mean(y * y, axis=-1, keepdims=True)
    yn = y * lax.rsqrt(ms + EPS) * gpost_ref[...]
    o_ref[...] = x_ref[...] + (1.0 + gate_ref[0]) * yn


def _out_proj(z2, diff_out, w_pool_bf, pool_scale, w_out_bf, x2, mod3, g_post, *, seq):
    M, D = x2.shape
    tm = 512
    per_seq = seq // tm
    band, hband = _pool_bands(tm)
    const = lambda shape: pl.BlockSpec(shape, lambda i: (0,) * len(shape))
    kern = functools.partial(_out_proj_kernel, tm=tm, per_seq=per_seq)
    return pl.pallas_call(
        kern,
        grid=(M // tm,),
        in_specs=[
            pl.BlockSpec((tm, D_POOL), lambda i: (i, 0)),
            pl.BlockSpec((POOL_HALO, D_POOL),
                         lambda i: (jnp.maximum(i * (tm // POOL_HALO) - 1, 0), 0)),
            pl.BlockSpec((tm, D_POOL), lambda i: (i, 1)),
            pl.BlockSpec((tm, D_DIFF), lambda i: (i, 0)),
            const(band.shape), const(hband.shape), const(w_pool_bf.shape), const((1, D_POOL)),
            const((D, D)),
            pl.BlockSpec((tm, D), lambda i: (i, 0)),
            pl.BlockSpec((1, 1, D), lambda i: (i // per_seq, 0, 2)),
            const((1, D)),
        ],
        out_specs=pl.BlockSpec((tm, D), lambda i: (i, 0)),
        out_shape=jax.ShapeDtypeStruct((M, D), F32),
        scratch_shapes=[pltpu.VMEM((tm, D_POOL), BF16)],
        compiler_params=pltpu.CompilerParams(
            dimension_semantics=("parallel",), vmem_limit_bytes=VMEM_LIMIT),
        name="out_proj",
    )(z2, z2, z2, diff_out, band, hband, w_pool_bf, pool_scale.reshape(1, D_POOL),
      w_out_bf, x2, mod3, g_post.reshape(1, D))


def _permute_qk_columns(w_in):
    depth, d, _ = w_in.shape
    lo, hi = 2 * D_POOL, 2 * D_POOL + 2 * D_QK
    half = DIFF_QK_DIM // 2
    qk = w_in[:, :, lo:hi].reshape(depth, d, 2 * DIFF_HEADS, 2, 2, half)
    qk = qk.transpose(0, 1, 2, 4, 3, 5).reshape(depth, d, 2 * D_QK)
    return jnp.concatenate([w_in[:, :, :lo], qk, w_in[:, :, hi:]], axis=-1)


def kernel(x, c, positions, w_ada, b_ada, g_pre, w_in, w_pool, pool_scale, lambda_q1, lambda_k1,
           lambda_q2, lambda_k2, subln_g, w_out, g_post):
    B, S, D = x.shape
    depth = w_in.shape[0]
    cos, sin = _rope_tables(positions)
    mod = _ada_modulation(c, w_ada, b_ada)
    w_in_bf = _permute_qk_columns(w_in).astype(BF16)
    w_pool_bf = w_pool.astype(BF16)
    w_out_bf = w_out.astype(BF16)
    x2 = x.reshape(B * S, D)
    for l in range(depth):
        lam_init = 0.8 - 0.6 * math.exp(-0.3 * l)
        mod3 = mod[l].reshape(B, 1, 3 * D)
        z2 = _in_proj(x2, mod3, g_pre[l], w_in_bf[l], cos, sin, seq=S)
        diff_out = _diff_attention(z2.reshape(B, S, D_IN), lambda_q1[l], lambda_k1[l],
                                   lambda_q2[l], lambda_k2[l], subln_g[l], lam_init=lam_init)
        x2 = _out_proj(z2, diff_out.reshape(B * S, D_DIFF), w_pool_bf[l], pool_scale[l],
                       w_out_bf[l], x2, mod3, g_post[l], seq=S)
    return x2.reshape(B, S, D)
```

```python
import functools
import math

import jax
import jax.numpy as jnp
from jax import lax
from jax.experimental import pallas as pl
from jax.experimental.pallas import tpu as pltpu
import numpy as np

D_MODEL = 2048
CHUNK = 64
D_POOL = 1024
D_DIFF = 1024
POOL_WINDOWS = (2, 4, 8, 16)
N_POOL_GROUPS = 4
POOL_GROUP_DIM = 256
DIFF_HEADS = 8
DIFF_V_DIM = 128
DIFF_QK_DIM = 64
D_QK = 1024
D_IN = 6144
ROPE_THETA = 10000.0
EPS = 1e-6
NEG_INF = -1e30

LANES = 128
ONES_ROWS = 16
ATTN_TQ = 512
POOL_HALO = 16
VMEM_LIMIT = 56 * 1024 * 1024
LOG2_E = math.log2(math.e)

F32 = jnp.float32
BF16 = jnp.bfloat16


def _silu(g):
    return g * jax.nn.sigmoid(g)


def _rope_table_kernel(pos_ref, freq_ref, sign_ref, cos_ref, sin_ref):
    ang = pos_ref[0].astype(F32) * freq_ref[...]
    cos_ref[0] = jnp.cos(ang)
    sin_ref[0] = jnp.sin(ang) * sign_ref[...]


def _rope_tables(positions):
    B, S = positions.shape
    ts = 512
    inv_freq = 1.0 / (ROPE_THETA ** (jnp.arange(0, DIFF_QK_DIM, 2, dtype=F32) / DIFF_QK_DIM))
    freq = jnp.tile(inv_freq, LANES // (DIFF_QK_DIM // 2))[None, :]
    sign = jnp.where(jnp.arange(LANES) < LANES // 2, -1.0, 1.0).astype(F32)[None, :]
    cos, sin = pl.pallas_call(
        _rope_table_kernel,
        grid=(B, S // ts),
        in_specs=[
            pl.BlockSpec((1, ts, 1), lambda b, i: (b, i, 0)),
            pl.BlockSpec((1, LANES), lambda b, i: (0, 0)),
            pl.BlockSpec((1, LANES), lambda b, i: (0, 0)),
        ],
        out_specs=[
            pl.BlockSpec((1, ts, LANES), lambda b, i: (b, i, 0)),
            pl.BlockSpec((1, ts, LANES), lambda b, i: (b, i, 0)),
        ],
        out_shape=[jax.ShapeDtypeStruct((B, S, LANES), F32)] * 2,
        name="rope_tables",
    )(positions.reshape(B, S, 1), freq, sign)
    return cos.reshape(B * S, LANES), sin.reshape(B * S, LANES)


def _ada_kernel(c_ref, w_ref, b_ref, o_ref):
    o_ref[0] = jnp.dot(c_ref[...], w_ref[0], preferred_element_type=F32) + b_ref[0]


def _ada_modulation(c, w_ada, b_ada):
    depth, d, n = w_ada.shape
    B = c.shape[0]
    rows = 8
    c_pad = jnp.zeros((rows, d), F32).at[:B].set(c)
    tn = 1024
    out = pl.pallas_call(
        _ada_kernel,
        grid=(depth, n // tn),
        in_specs=[
            pl.BlockSpec((rows, d), lambda l, j: (0, 0)),
            pl.BlockSpec((1, d, tn), lambda l, j: (l, 0, j)),
            pl.BlockSpec((1, 1, tn), lambda l, j: (l, 0, j)),
        ],
        out_specs=pl.BlockSpec((1, rows, tn), lambda l, j: (l, 0, j)),
        out_shape=jax.ShapeDtypeStruct((depth, rows, n), F32),
        compiler_params=pltpu.CompilerParams(vmem_limit_bytes=VMEM_LIMIT),
        name="ada_modulation",
    )(c_pad, w_ada, b_ada.reshape(depth, 1, n))
    return out[:, :B]


def _in_proj_kernel(x_ref, shift_ref, scale_ref, gpre_ref, w_ref, cos_ref, sin_ref,
                    z_ref, h_ref, *, tm, tn):
    j = pl.program_id(1)

    @pl.when(j == 0)
    def _():
        x = x_ref[...]
        ms = jnp.mean(x * x, axis=-1, keepdims=True)
        y = x * lax.rsqrt(ms + EPS) * gpre_ref[...]
        h_ref[...] = (y * (1.0 + scale_ref[0]) + shift_ref[0]).astype(BF16)

    acc = jnp.dot(h_ref[...], w_ref[...], preferred_element_type=F32)
    q_tile = (2 * D_POOL) // tn
    k_tile = (2 * D_POOL + D_QK) // tn
    is_rope = jnp.logical_or(j == q_tile, j == k_tile)

    @pl.when(is_rope)
    def _():
        qk_scale = jnp.where(j == q_tile, LOG2_E / math.sqrt(DIFF_QK_DIM), 1.0).astype(F32)
        cos = cos_ref[...] * qk_scale
        sin = sin_ref[...] * qk_scale
        for g in range(tn // LANES):
            xg = acc[:, g * LANES:(g + 1) * LANES]
            rot = pltpu.roll(xg, LANES // 2, 1)
            z_ref[:, g * LANES:(g + 1) * LANES] = (xg * cos + rot * sin).astype(BF16)

    @pl.when(jnp.logical_not(is_rope))
    def _():
        z_ref[...] = acc.astype(BF16)


def _in_proj(x2, mod3, g_pre, w_in_bf, cos, sin, *, seq, layer):
    M, D = x2.shape
    N = w_in_bf.shape[2]
    tm, tn = 1024, 1024
    per_seq = seq // tm
    kern = functools.partial(_in_proj_kernel, tm=tm, tn=tn)
    return pl.pallas_call(
        kern,
        grid=(M // tm, N // tn),
        in_specs=[
            pl.BlockSpec((tm, D), lambda i, j: (i, 0)),
            pl.BlockSpec((1, 1, D), lambda i, j: (i // per_seq, 0, 0)),
            pl.BlockSpec((1, 1, D), lambda i, j: (i // per_seq, 0, 1)),
            pl.BlockSpec((1, D), lambda i, j: (0, 0)),
            pl.BlockSpec((None, D, tn), lambda i, j: (layer, 0, j)),
            pl.BlockSpec((tm, LANES), lambda i, j: (i, 0)),
            pl.BlockSpec((tm, LANES), lambda i, j: (i, 0)),
        ],
        out_specs=pl.BlockSpec((tm, tn), lambda i, j: (i, j)),
        out_shape=jax.ShapeDtypeStruct((M, N), BF16),
        scratch_shapes=[pltpu.VMEM((tm, D), BF16)],
        compiler_params=pltpu.CompilerParams(
            dimension_semantics=("parallel", "arbitrary"), vmem_limit_bytes=VMEM_LIMIT),
        name="in_proj",
    )(x2, mod3, mod3, g_pre.reshape(1, D), w_in_bf, cos, sin)


def _attn_kernel(lq1_ref, lk1_ref, lq2_ref, lk2_ref, subg_ref, q_ref, k_ref, v_ref, g_ref,
                 o_ref, vt_ref, qcat_ref, acc_ref, *, seq, tq, lam_init):
    tk, hq, nq = tq, tq // 2, seq // tq
    lam = (jnp.exp(jnp.sum(lq1_ref[...] * lk1_ref[...], axis=-1, keepdims=True))
           - jnp.exp(jnp.sum(lq2_ref[...] * lk2_ref[...], axis=-1, keepdims=True))
           + lam_init)

    lane = lax.broadcasted_iota(jnp.int32, (tq, LANES), 1)
    first_map = (lane // (DIFF_QK_DIM // 2)) % 2 == 0
    for c in range(nq):
        rows = slice(c * tq, (c + 1) * tq)
        vt_ref[c, :DIFF_V_DIM, :] = v_ref[rows, :].astype(F32).T.astype(BF16)
        vt_ref[c, DIFF_V_DIM:, :] = jnp.ones((ONES_ROWS, tk), BF16)
        q = q_ref[rows, :]
        zero = jnp.zeros_like(q)
        q0, q1 = jnp.where(first_map, q, zero), jnp.where(first_map, zero, q)
        qcat_ref[c] = jnp.concatenate([q0[:hq], q1[:hq], q0[hq:], q1[hq:]], axis=0)

    key_a = lax.broadcasted_iota(jnp.int32, (hq, 2 * tq), 0)
    col_a = lax.broadcasted_iota(jnp.int32, (hq, 2 * tq), 1)
    mask_a = key_a // CHUNK <= (col_a % hq + hq * (col_a // tq)) // CHUNK
    key_b = lax.broadcasted_iota(jnp.int32, (hq, tq), 0)
    col_b = lax.broadcasted_iota(jnp.int32, (hq, tq), 1)
    mask_b = key_b // CHUNK <= (col_b % hq) // CHUNK
    contract_last = (((1,), (1,)), ((), ()))

    def scores(tile):
        kind, i, j = tile
        if kind == "full":
            keys, qc = k_ref[j * tk:(j + 1) * tk, :], qcat_ref[i]
        elif kind == "diag_a":
            keys, qc = k_ref[i * tk:i * tk + hq, :], qcat_ref[i]
        else:
            keys, qc = k_ref[i * tk + hq:(i + 1) * tk, :], qcat_ref[i, tq:, :]
        return lax.dot_general(keys, qc, contract_last, preferred_element_type=F32)

    def softmax_pv(tile, s, m):
        kind, i, j = tile
        acc = acc_ref.at[i % 2]
        if kind == "full":
            vt, cols = vt_ref[j], slice(None)
        elif kind == "diag_a":
            s = jnp.where(mask_a, s, NEG_INF)
            vt, cols = vt_ref[i, :, :hq], slice(None)
        else:
            s = jnp.where(mask_b, s, NEG_INF)
            vt, cols, m = vt_ref[i, :, hq:], slice(tq, None), m[:, tq:]
        m_new = jnp.max(s, axis=0, keepdims=True)
        if m is not None:
            m_new = jnp.maximum(m, m_new)
        p = jnp.exp2(s - m_new).astype(BF16)
        pv = jnp.dot(vt, p, preferred_element_type=F32)
        if m is None:
            acc[...] = pv
        else:
            acc[:, cols] = acc[:, cols] * jnp.exp2(m - m_new) + pv
        return m_new

    def finish(i):
        acc = acc_ref.at[i % 2]
        o = acc[:DIFF_V_DIM, :] / acc[DIFF_V_DIM:DIFF_V_DIM + 1, :]
        o0 = jnp.concatenate([o[:, :hq], o[:, tq:tq + hq]], axis=1)
        o1 = jnp.concatenate([o[:, hq:tq], o[:, tq + hq:]], axis=1)
        att = (o0 - lam * o1).T
        ms = jnp.mean(att * att, axis=-1, keepdims=True)
        y = att * lax.rsqrt(ms + EPS) * subg_ref[...] * (1.0 - lam_init)
        rows = slice(i * tq, (i + 1) * tq)
        o_ref[rows, :] = (y * _silu(g_ref[rows, :].astype(F32))).astype(BF16)

    tiles = []
    for i in range(nq):
        tiles += [("full", i, j) for j in range(i)] + [("diag_a", i, i), ("diag_b", i, i)]
    s_next = scores(tiles[0])
    m = None
    for n, tile in enumerate(tiles):
        s = s_next
        if n + 1 < len(tiles):
            s_next = scores(tiles[n + 1])
        m = softmax_pv(tile, s, m)
        if tile[0] == "diag_b":
            finish(tile[1])
            m = None


def _diff_attention(z3, lq1, lk1, lq2, lk2, subln_g, *, lam_init):
    B, S, _ = z3.shape
    tq = ATTN_TQ
    q_col = (2 * D_POOL) // LANES
    k_col = q_col + D_QK // LANES
    v_col = k_col + D_QK // LANES
    g_col = v_col + D_DIFF // LANES
    vec = lambda n: pl.BlockSpec((1, n), lambda b, h: (0, 0))
    head = lambda col: pl.BlockSpec((None, S, LANES), lambda b, h: (b, 0, col + h))
    kern = functools.partial(_attn_kernel, seq=S, tq=tq, lam_init=lam_init)
    return pl.pallas_call(
        kern,
        grid=(B, DIFF_HEADS),
        in_specs=[vec(DIFF_QK_DIM)] * 4 + [vec(DIFF_V_DIM),
                  head(q_col), head(k_col), head(v_col), head(g_col)],
        out_specs=pl.BlockSpec((None, S, LANES), lambda b, h: (b, 0, h)),
        out_shape=jax.ShapeDtypeStruct((B, S, D_DIFF), BF16),
        scratch_shapes=[pltpu.VMEM((S // tq, DIFF_V_DIM + ONES_ROWS, tq), BF16),
                        pltpu.VMEM((S // tq, 2 * tq, LANES), BF16),
                        pltpu.VMEM((2, DIFF_V_DIM + ONES_ROWS, 2 * tq), F32)],
        compiler_params=pltpu.CompilerParams(
            dimension_semantics=("parallel", "parallel"), vmem_limit_bytes=VMEM_LIMIT),
        name="diff_attention",
    )(lq1.reshape(1, -1), lk1.reshape(1, -1), lq2.reshape(1, -1), lk2.reshape(1, -1),
      subln_g.reshape(1, -1), z3, z3, z3, z3)


def _pool_bands(tm):
    lag = np.arange(tm)[:, None] - np.arange(tm)[None, :]
    hlag = np.arange(POOL_HALO)[:, None] + POOL_HALO - np.arange(POOL_HALO)[None, :]
    band = np.stack([(lag >= 0) & (lag < w) for w in POOL_WINDOWS]).astype(np.float32)
    hband = np.stack([hlag < w for w in POOL_WINDOWS]).astype(np.float32)
    return jnp.asarray(band, BF16), jnp.asarray(hband, BF16)


def _out_proj_kernel(u_ref, halo_ref, gp_ref, d_ref, band_ref, hband_ref, wp_ref, ps_ref,
                     w_ref, x_ref, gate_ref, gpost_ref, o_ref, pool_ref, *, tm, per_seq):
    it = pl.program_id(0) % per_seq
    t1 = it * tm + lax.broadcasted_iota(jnp.int32, (tm, 1), 0) + 1
    y = jnp.dot(d_ref[...], w_ref[D_POOL:, :], preferred_element_type=F32)

    gd = POOL_GROUP_DIM
    for g, window in enumerate(POOL_WINDOWS):
        cols = slice(g * gd, (g + 1) * gd)
        u = u_ref[:, cols]
        win = jnp.dot(band_ref[g], u, preferred_element_type=F32)
        hwin = jnp.dot(hband_ref[g], halo_ref[:, cols], preferred_element_type=F32)
        hwin = jnp.where(it == 0, 0.0, hwin)
        win = jnp.concatenate([win[:POOL_HALO] + hwin, win[POOL_HALO:]], axis=0)
        count = jnp.minimum(t1, window).astype(F32)
        pooled = win / count - u.astype(F32)
        mixed = jnp.dot(pooled.astype(BF16), wp_ref[g], preferred_element_type=F32) * ps_ref[:, cols]
        pool_ref[:, cols] = (mixed * _silu(gp_ref[:, cols].astype(F32))).astype(BF16)

    y = y + jnp.dot(pool_ref[...], w_ref[:D_POOL, :], preferred_element_type=F32)
    ms = jnp.mean(y * y, axis=-1, keepdims=True)
    yn = y * lax.rsqrt(ms + EPS) * gpost_ref[...]
    o_ref[...] = x_ref[...] + (1.0 + gate_ref[0]) * yn


def _out_proj(z2, diff_out, w_pool_bf, pool_scale, w_out_bf, x2, mod3, g_post, *, seq, layer):
    M, D = x2.shape
    tm = 512
    per_seq = seq // tm
    band, hband = _pool_bands(tm)
    const = lambda shape: pl.BlockSpec(shape, lambda i: (0,) * len(shape))
    of_layer = lambda shape: pl.BlockSpec((None,) + shape, lambda i: (layer,) + (0,) * len(shape))
    kern = functools.partial(_out_proj_kernel, tm=tm, per_seq=per_seq)
    return pl.pallas_call(
        kern,
        grid=(M // tm,),
        in_specs=[
            pl.BlockSpec((tm, D_POOL), lambda i: (i, 0)),
            pl.BlockSpec((POOL_HALO, D_POOL),
                         lambda i: (jnp.maximum(i * (tm // POOL_HALO) - 1, 0), 0)),
            pl.BlockSpec((tm, D_POOL), lambda i: (i, 1)),
            pl.BlockSpec((tm, D_DIFF), lambda i: (i, 0)),
            const(band.shape), const(hband.shape), of_layer(w_pool_bf.shape[1:]),
            const((1, D_POOL)), of_layer((D, D)),
            pl.BlockSpec((tm, D), lambda i: (i, 0)),
            pl.BlockSpec((1, 1, D), lambda i: (i // per_seq, 0, 2)),
            const((1, D)),
        ],
        out_specs=pl.BlockSpec((tm, D), lambda i: (i, 0)),
        out_shape=jax.ShapeDtypeStruct((M, D), F32),
        scratch_shapes=[pltpu.VMEM((tm, D_POOL), BF16)],
        compiler_params=pltpu.CompilerParams(
            dimension_semantics=("parallel",), vmem_limit_bytes=VMEM_LIMIT),
        name="out_proj",
    )(z2, z2, z2, diff_out, band, hband, w_pool_bf, pool_scale.reshape(1, D_POOL),
      w_out_bf, x2, mod3, g_post.reshape(1, D))


def _w_in_prep_kernel(w_ref, o_ref, *, tn):
    j = pl.program_id(1)
    first_qk = (2 * D_POOL) // tn
    is_qk = jnp.logical_and(j >= first_qk, j < first_qk + (2 * D_QK) // tn)

    @pl.when(is_qk)
    def _():
        quarter = DIFF_QK_DIM // 2
        blk = lax.broadcasted_iota(jnp.int32, (w_ref.shape[0], LANES), 1) // quarter
        for g in range(tn // LANES):
            x = w_ref[:, g * LANES:(g + 1) * LANES]
            y = jnp.where(blk == 1, pltpu.roll(x, LANES - quarter, 1),
                          jnp.where(blk == 2, pltpu.roll(x, quarter, 1), x))
            o_ref[:, g * LANES:(g + 1) * LANES] = y.astype(BF16)

    @pl.when(jnp.logical_not(is_qk))
    def _():
        o_ref[...] = w_ref[...].astype(BF16)


def _w_in_prep(w_in):
    depth, d, n = w_in.shape
    tn = 512
    return pl.pallas_call(
        functools.partial(_w_in_prep_kernel, tn=tn),
        grid=(depth, n // tn),
        in_specs=[pl.BlockSpec((None, d, tn), lambda l, j: (l, 0, j))],
        out_specs=pl.BlockSpec((None, d, tn), lambda l, j: (l, 0, j)),
        out_shape=jax.ShapeDtypeStruct((depth, d, n), BF16),
        compiler_params=pltpu.CompilerParams(
            dimension_semantics=("parallel", "parallel"), vmem_limit_bytes=VMEM_LIMIT),
        name="w_in_prep",
    )(w_in)


def kernel(x, c, positions, w_ada, b_ada, g_pre, w_in, w_pool, pool_scale, lambda_q1, lambda_k1,
           lambda_q2, lambda_k2, subln_g, w_out, g_post):
    B, S, D = x.shape
    depth = w_in.shape[0]
    cos, sin = _rope_tables(positions)
    mod = _ada_modulation(c, w_ada, b_ada)
    w_in_bf = _w_in_prep(w_in)
    w_pool_bf = w_pool.astype(BF16)
    w_out_bf = w_out.astype(BF16)
    x2 = x.reshape(B * S, D)
    for l in range(depth):
        lam_init = 0.8 - 0.6 * math.exp(-0.3 * l)
        mod3 = mod[l].reshape(B, 1, 3 * D)
        z2 = _in_proj(x2, mod3, g_pre[l], w_in_bf, cos, sin, seq=S, layer=l)
        diff_out = _diff_attention(z2.reshape(B, S, D_IN), lambda_q1[l], lambda_k1[l],
                                   lambda_q2[l], lambda_k2[l], subln_g[l], lam_init=lam_init)
        x2 = _out_proj(z2, diff_out.reshape(B * S, D_DIFF), w_pool_bf, pool_scale[l],
                       w_out_bf, x2, mod3, g_post[l], seq=S, layer=l)
    return x2.reshape(B, S, D)
```

```python
import functools
import math

import jax
import jax.numpy as jnp
from jax import lax
from jax.experimental import pallas as pl
from jax.experimental.pallas import tpu as pltpu
import numpy as np

D_MODEL = 2048
CHUNK = 64
D_POOL = 1024
D_DIFF = 1024
POOL_WINDOWS = (2, 4, 8, 16)
N_POOL_GROUPS = 4
POOL_GROUP_DIM = 256
DIFF_HEADS = 8
DIFF_V_DIM = 128
DIFF_QK_DIM = 64
D_QK = 1024
D_IN = 6144
ROPE_THETA = 10000.0
EPS = 1e-6
NEG_INF = -1e30

LANES = 128
ONES_ROWS = 16
ATTN_TQ = 512
POOL_HALO = 16
VMEM_LIMIT = 56 * 1024 * 1024
LOG2_E = math.log2(math.e)

F32 = jnp.float32
BF16 = jnp.bfloat16


def _silu(g):
    return g * jax.nn.sigmoid(g)


def _rope_table_kernel(pos_ref, freq_ref, sign_ref, cos_ref, sin_ref):
    ang = pos_ref[0].astype(F32) * freq_ref[...]
    cos_ref[0] = jnp.cos(ang)
    sin_ref[0] = jnp.sin(ang) * sign_ref[...]


def _rope_tables(positions):
    B, S = positions.shape
    ts = 512
    inv_freq = 1.0 / (ROPE_THETA ** (jnp.arange(0, DIFF_QK_DIM, 2, dtype=F32) / DIFF_QK_DIM))
    freq = jnp.tile(inv_freq, LANES // (DIFF_QK_DIM // 2))[None, :]
    sign = jnp.where(jnp.arange(LANES) < LANES // 2, -1.0, 1.0).astype(F32)[None, :]
    cos, sin = pl.pallas_call(
        _rope_table_kernel,
        grid=(B, S // ts),
        in_specs=[
            pl.BlockSpec((1, ts, 1), lambda b, i: (b, i, 0)),
            pl.BlockSpec((1, LANES), lambda b, i: (0, 0)),
            pl.BlockSpec((1, LANES), lambda b, i: (0, 0)),
        ],
        out_specs=[
            pl.BlockSpec((1, ts, LANES), lambda b, i: (b, i, 0)),
            pl.BlockSpec((1, ts, LANES), lambda b, i: (b, i, 0)),
        ],
        out_shape=[jax.ShapeDtypeStruct((B, S, LANES), F32)] * 2,
        name="rope_tables",
    )(positions.reshape(B, S, 1), freq, sign)
    return cos.reshape(B * S, LANES), sin.reshape(B * S, LANES)


def _ada_kernel(c_ref, w_ref, b_ref, o_ref):
    o_ref[0] = jnp.dot(c_ref[...], w_ref[0], preferred_element_type=F32) + b_ref[0]


def _ada_modulation(c, w_ada, b_ada):
    depth, d, n = w_ada.shape
    B = c.shape[0]
    rows = 8
    c_pad = jnp.zeros((rows, d), F32).at[:B].set(c)
    tn = 1024
    out = pl.pallas_call(
        _ada_kernel,
        grid=(depth, n // tn),
        in_specs=[
            pl.BlockSpec((rows, d), lambda l, j: (0, 0)),
            pl.BlockSpec((1, d, tn), lambda l, j: (l, 0, j)),
            pl.BlockSpec((1, 1, tn), lambda l, j: (l, 0, j)),
        ],
        out_specs=pl.BlockSpec((1, rows, tn), lambda l, j: (l, 0, j)),
        out_shape=jax.ShapeDtypeStruct((depth, rows, n), F32),
        compiler_params=pltpu.CompilerParams(vmem_limit_bytes=VMEM_LIMIT),
        name="ada_modulation",
    )(c_pad, w_ada, b_ada.reshape(depth, 1, n))
    return out[:, :B]


def _in_proj_kernel(x0_ref, xn_ref, shift0_ref, scale0_ref, shiftn_ref, scalen_ref, gpre_ref,
                    w_ref, cos_ref, sin_ref, z_ref, h_ref, *, tn):
    i = pl.program_id(0)
    n_col_tiles = w_ref.shape[1] // tn
    q_tile = (2 * D_POOL) // tn
    k_tile = (2 * D_POOL + D_QK) // tn

    def normed(x_ref, shift_ref, scale_ref):
        x = x_ref[...]
        ms = jnp.mean(x * x, axis=-1, keepdims=True)
        y = x * lax.rsqrt(ms + EPS) * gpre_ref[...]
        return (y * (1.0 + scale_ref[0]) + shift_ref[0]).astype(BF16)

    @pl.when(i == 0)
    def _():
        h_ref[0] = normed(x0_ref, shift0_ref, scale0_ref)

    def step(cur):
        h_ref[1 - cur] = normed(xn_ref, shiftn_ref, scalen_ref)
        h = h_ref[cur]
        for jt in range(n_col_tiles):
            acc = jnp.dot(h, w_ref[:, jt * tn:(jt + 1) * tn], preferred_element_type=F32)
            if jt in (q_tile, k_tile):
                qk_scale = LOG2_E / math.sqrt(DIFF_QK_DIM) if jt == q_tile else 1.0
                cos = cos_ref[...] * qk_scale
                sin = sin_ref[...] * qk_scale
                for g in range(tn // LANES):
                    xg = acc[:, g * LANES:(g + 1) * LANES]
                    rot = pltpu.roll(xg, LANES // 2, 1)
                    col = jt * tn + g * LANES
                    z_ref[:, col:col + LANES] = (xg * cos + rot * sin).astype(BF16)
            else:
                z_ref[:, jt * tn:(jt + 1) * tn] = acc.astype(BF16)

    for parity in range(2):
        pl.when(i % 2 == parity)(functools.partial(step, parity))


def _in_proj(x2, mod3, g_pre, w_in_bf, cos, sin, *, seq, layer):
    M, D = x2.shape
    N = w_in_bf.shape[2]
    tm, tn = 256, 1024
    per_seq = seq // tm
    last = M // tm - 1
    nxt = lambda i: jnp.minimum(i + 1, last)
    kern = functools.partial(_in_proj_kernel, tn=tn)
    return pl.pallas_call(
        kern,
        grid=(M // tm,),
        in_specs=[
            pl.BlockSpec((tm, D), lambda i: (0, 0)),
            pl.BlockSpec((tm, D), lambda i: (nxt(i), 0)),
            pl.BlockSpec((1, 1, D), lambda i: (0, 0, 0)),
            pl.BlockSpec((1, 1, D), lambda i: (0, 0, 1)),
            pl.BlockSpec((1, 1, D), lambda i: (nxt(i) // per_seq, 0, 0)),
            pl.BlockSpec((1, 1, D), lambda i: (nxt(i) // per_seq, 0, 1)),
            pl.BlockSpec((1, D), lambda i: (0, 0)),
            pl.BlockSpec((None, D, N), lambda i: (layer, 0, 0), pipeline_mode=pl.Buffered(1)),
            pl.BlockSpec((tm, LANES), lambda i: (i, 0)),
            pl.BlockSpec((tm, LANES), lambda i: (i, 0)),
        ],
        out_specs=pl.BlockSpec((tm, N), lambda i: (i, 0)),
        out_shape=jax.ShapeDtypeStruct((M, N), BF16),
        scratch_shapes=[pltpu.VMEM((2, tm, D), BF16)],
        compiler_params=pltpu.CompilerParams(
            dimension_semantics=("arbitrary",), vmem_limit_bytes=VMEM_LIMIT),
        name="in_proj",
    )(x2, x2, mod3, mod3, mod3, mod3, g_pre.reshape(1, D), w_in_bf, cos, sin)


def _attn_kernel(lq1_ref, lk1_ref, lq2_ref, lk2_ref, subg_ref, q_ref, k_ref, v_ref, g_ref,
                 o_ref, vt_ref, qcat_ref, acc_ref, *, seq, tq, lam_init):
    tk, hq, nq = tq, tq // 2, seq // tq
    lam = (jnp.exp(jnp.sum(lq1_ref[...] * lk1_ref[...], axis=-1, keepdims=True))
           - jnp.exp(jnp.sum(lq2_ref[...] * lk2_ref[...], axis=-1, keepdims=True))
           + lam_init)

    lane = lax.broadcasted_iota(jnp.int32, (tq, LANES), 1)
    first_map = (lane // (DIFF_QK_DIM // 2)) % 2 == 0
    for c in range(nq):
        rows = slice(c * tq, (c + 1) * tq)
        vt_ref[c, :DIFF_V_DIM, :] = v_ref[rows, :].astype(F32).T.astype(BF16)
        vt_ref[c, DIFF_V_DIM:, :] = jnp.ones((ONES_ROWS, tk), BF16)
        q = q_ref[rows, :]
        zero = jnp.zeros_like(q)
        q0, q1 = jnp.where(first_map, q, zero), jnp.where(first_map, zero, q)
        qcat_ref[c] = jnp.concatenate([q0[:hq], q1[:hq], q0[hq:], q1[hq:]], axis=0)

    key_a = lax.broadcasted_iota(jnp.int32, (hq, 2 * tq), 0)
    col_a = lax.broadcasted_iota(jnp.int32, (hq, 2 * tq), 1)
    mask_a = key_a // CHUNK <= (col_a % hq + hq * (col_a // tq)) // CHUNK
    key_b = lax.broadcasted_iota(jnp.int32, (hq, tq), 0)
    col_b = lax.broadcasted_iota(jnp.int32, (hq, tq), 1)
    mask_b = key_b // CHUNK <= (col_b % hq) // CHUNK
    contract_last = (((1,), (1,)), ((), ()))

    def scores(tile):
        kind, i, j = tile
        if kind == "full":
            keys, qc = k_ref[j * tk:(j + 1) * tk, :], qcat_ref[i]
        elif kind == "diag_a":
            keys, qc = k_ref[i * tk:i * tk + hq, :], qcat_ref[i]
        else:
            keys, qc = k_ref[i * tk + hq:(i + 1) * tk, :], qcat_ref[i, tq:, :]
        return lax.dot_general(keys, qc, contract_last, preferred_element_type=F32)

    def softmax_pv(tile, s, m):
        kind, i, j = tile
        acc = acc_ref.at[i % 2]
        if kind == "full":
            vt, cols = vt_ref[j], slice(None)
        elif kind == "diag_a":
            s = jnp.where(mask_a, s, NEG_INF)
            vt, cols = vt_ref[i, :, :hq], slice(None)
        else:
            s = jnp.where(mask_b, s, NEG_INF)
            vt, cols, m = vt_ref[i, :, hq:], slice(tq, None), m[:, tq:]
        m_new = jnp.max(s, axis=0, keepdims=True)
        if m is not None:
            m_new = jnp.maximum(m, m_new)
        p = jnp.exp2(s - m_new).astype(BF16)
        pv = jnp.dot(vt, p, preferred_element_type=F32)
        if m is None:
            acc[...] = pv
        else:
            acc[:, cols] = acc[:, cols] * jnp.exp2(m - m_new) + pv
        return m_new

    def finish(i):
        acc = acc_ref.at[i % 2]
        o = acc[:DIFF_V_DIM, :] / acc[DIFF_V_DIM:DIFF_V_DIM + 1, :]
        o0 = jnp.concatenate([o[:, :hq], o[:, tq:tq + hq]], axis=1)
        o1 = jnp.concatenate([o[:, hq:tq], o[:, tq + hq:]], axis=1)
        att = (o0 - lam * o1).T
        ms = jnp.mean(att * att, axis=-1, keepdims=True)
        y = att * lax.rsqrt(ms + EPS) * subg_ref[...] * (1.0 - lam_init)
        rows = slice(i * tq, (i + 1) * tq)
        o_ref[rows, :] = (y * _silu(g_ref[rows, :].astype(F32))).astype(BF16)

    tiles = []
    for i in range(nq):
        tiles += [("full", i, j) for j in range(i)] + [("diag_a", i, i), ("diag_b", i, i)]
    s_next = scores(tiles[0])
    m = None
    for n, tile in enumerate(tiles):
        s = s_next
        if n + 1 < len(tiles):
            s_next = scores(tiles[n + 1])
        m = softmax_pv(tile, s, m)
        if tile[0] == "diag_b":
            finish(tile[1])
            m = None


def _diff_attention(z3, lq1, lk1, lq2, lk2, subln_g, *, lam_init):
    B, S, _ = z3.shape
    tq = ATTN_TQ
    q_col = (2 * D_POOL) // LANES
    k_col = q_col + D_QK // LANES
    v_col = k_col + D_QK // LANES
    g_col = v_col + D_DIFF // LANES
    vec = lambda n: pl.BlockSpec((1, n), lambda b, h: (0, 0))
    head = lambda col: pl.BlockSpec((None, S, LANES), lambda b, h: (b, 0, col + h))
    kern = functools.partial(_attn_kernel, seq=S, tq=tq, lam_init=lam_init)
    return pl.pallas_call(
        kern,
        grid=(B, DIFF_HEADS),
        in_specs=[vec(DIFF_QK_DIM)] * 4 + [vec(DIFF_V_DIM),
                  head(q_col), head(k_col), head(v_col), head(g_col)],
        out_specs=pl.BlockSpec((None, S, LANES), lambda b, h: (b, 0, h)),
        out_shape=jax.ShapeDtypeStruct((B, S, D_DIFF), BF16),
        scratch_shapes=[pltpu.VMEM((S // tq, DIFF_V_DIM + ONES_ROWS, tq), BF16),
                        pltpu.VMEM((S // tq, 2 * tq, LANES), BF16),
                        pltpu.VMEM((2, DIFF_V_DIM + ONES_ROWS, 2 * tq), F32)],
        compiler_params=pltpu.CompilerParams(
            dimension_semantics=("parallel", "parallel"), vmem_limit_bytes=VMEM_LIMIT),
        name="diff_attention",
    )(lq1.reshape(1, -1), lk1.reshape(1, -1), lq2.reshape(1, -1), lk2.reshape(1, -1),
      subln_g.reshape(1, -1), z3, z3, z3, z3)


def _pool_bands(tm):
    lag = np.arange(tm)[:, None] - np.arange(tm)[None, :]
    hlag = np.arange(POOL_HALO)[:, None] + POOL_HALO - np.arange(POOL_HALO)[None, :]
    band = np.stack([(lag >= 0) & (lag < w) for w in POOL_WINDOWS]).astype(np.float32)
    hband = np.stack([hlag < w for w in POOL_WINDOWS]).astype(np.float32)
    return jnp.asarray(band, BF16), jnp.asarray(hband, BF16)


def _out_proj_kernel(u_ref, halo_ref, gp_ref, d_ref, band_ref, hband_ref, wp_ref, ps_ref,
                     w_ref, x_ref, gate_ref, gpost_ref, o_ref, pool_ref, *, tm, per_seq):
    it = pl.program_id(0) % per_seq
    sub = band_ref.shape[1]
    gd = POOL_GROUP_DIM
    for r in range(tm // sub):
        rows = slice(r * sub, (r + 1) * sub)
        t1 = it * tm + r * sub + lax.broadcasted_iota(jnp.int32, (sub, 1), 0) + 1
        y = jnp.dot(d_ref[rows, :], w_ref[D_POOL:, :], preferred_element_type=F32)
        for g, window in enumerate(POOL_WINDOWS):
            cols = slice(g * gd, (g + 1) * gd)
            u = u_ref[rows, cols]
            win = jnp.dot(band_ref[g], u, preferred_element_type=F32)
            if r == 0:
                hwin = jnp.dot(hband_ref[g], halo_ref[:, cols], preferred_element_type=F32)
                hwin = jnp.where(it == 0, 0.0, hwin)
            else:
                hwin = jnp.dot(hband_ref[g], u_ref[r * sub - POOL_HALO:r * sub, cols],
                               preferred_element_type=F32)
            win = jnp.concatenate([win[:POOL_HALO] + hwin, win[POOL_HALO:]], axis=0)
            count = jnp.minimum(t1, window).astype(F32)
            pooled = win / count - u.astype(F32)
            mixed = (jnp.dot(pooled.astype(BF16), wp_ref[g], preferred_element_type=F32)
                     * ps_ref[:, cols])
            pool_ref[rows, cols] = (mixed * _silu(gp_ref[rows, cols].astype(F32))).astype(BF16)

        y = y + jnp.dot(pool_ref[rows, :], w_ref[:D_POOL, :], preferred_element_type=F32)
        ms = jnp.mean(y * y, axis=-1, keepdims=True)
        yn = y * lax.rsqrt(ms + EPS) * gpost_ref[...]
        o_ref[rows, :] = x_ref[rows, :] + (1.0 + gate_ref[0]) * yn


def _out_proj(z2, diff_out, w_pool_bf, pool_scale, w_out_bf, x2, mod3, g_post, *, seq, layer):
    M, D = x2.shape
    tm = 512
    per_seq = seq // tm
    band, hband = _pool_bands(tm // 2)
    const = lambda shape: pl.BlockSpec(shape, lambda i: (0,) * len(shape))
    of_layer = lambda shape: pl.BlockSpec((None,) + shape, lambda i: (layer,) + (0,) * len(shape))
    kern = functools.partial(_out_proj_kernel, tm=tm, per_seq=per_seq)
    return pl.pallas_call(
        kern,
        grid=(M // tm,),
        in_specs=[
            pl.BlockSpec((tm, D_POOL), lambda i: (i, 0)),
            pl.BlockSpec((POOL_HALO, D_POOL),
                         lambda i: (jnp.maximum(i * (tm // POOL_HALO) - 1, 0), 0)),
            pl.BlockSpec((tm, D_POOL), lambda i: (i, 1)),
            pl.BlockSpec((tm, D_DIFF), lambda i: (i, 0)),
            const(band.shape), const(hband.shape), of_layer(w_pool_bf.shape[1:]),
            const((1, D_POOL)), of_layer((D, D)),
            pl.BlockSpec((tm, D), lambda i: (i, 0)),
            pl.BlockSpec((1, 1, D), lambda i: (i // per_seq, 0, 2)),
            const((1, D)),
        ],
        out_specs=pl.BlockSpec((tm, D), lambda i: (i, 0)),
        out_shape=jax.ShapeDtypeStruct((M, D), F32),
        scratch_shapes=[pltpu.VMEM((tm, D_POOL), BF16)],
        compiler_params=pltpu.CompilerParams(
            dimension_semantics=("parallel",), vmem_limit_bytes=VMEM_LIMIT),
        name="out_proj",
    )(z2, z2, z2, diff_out, band, hband, w_pool_bf, pool_scale.reshape(1, D_POOL),
      w_out_bf, x2, mod3, g_post.reshape(1, D))


def _w_in_prep_kernel(w_ref, o_ref, *, tn):
    j = pl.program_id(1)
    first_qk = (2 * D_POOL) // tn
    is_qk = jnp.logical_and(j >= first_qk, j < first_qk + (2 * D_QK) // tn)

    @pl.when(is_qk)
    def _():
        quarter = DIFF_QK_DIM // 2
        blk = lax.broadcasted_iota(jnp.int32, (w_ref.shape[0], LANES), 1) // quarter
        for g in range(tn // LANES):
            x = w_ref[:, g * LANES:(g + 1) * LANES]
            y = jnp.where(blk == 1, pltpu.roll(x, LANES - quarter, 1),
                          jnp.where(blk == 2, pltpu.roll(x, quarter, 1), x))
            o_ref[:, g * LANES:(g + 1) * LANES] = y.astype(BF16)

    @pl.when(jnp.logical_not(is_qk))
    def _():
        o_ref[...] = w_ref[...].astype(BF16)


def _w_in_prep(w_in):
    depth, d, n = w_in.shape
    tn = 512
    return pl.pallas_call(
        functools.partial(_w_in_prep_kernel, tn=tn),
        grid=(depth, n // tn),
        in_specs=[pl.BlockSpec((None, d, tn), lambda l, j: (l, 0, j))],
        out_specs=pl.BlockSpec((None, d, tn), lambda l, j: (l, 0, j)),
        out_shape=jax.ShapeDtypeStruct((depth, d, n), BF16),
        compiler_params=pltpu.CompilerParams(
            dimension_semantics=("parallel", "parallel"), vmem_limit_bytes=VMEM_LIMIT),
        name="w_in_prep",
    )(w_in)


def kernel(x, c, positions, w_ada, b_ada, g_pre, w_in, w_pool, pool_scale, lambda_q1, lambda_k1,
           lambda_q2, lambda_k2, subln_g, w_out, g_post):
    B, S, D = x.shape
    depth = w_in.shape[0]
    cos, sin = _rope_tables(positions)
    mod = _ada_modulation(c, w_ada, b_ada)
    w_in_bf = _w_in_prep(w_in)
    w_pool_bf = w_pool.astype(BF16)
    w_out_bf = w_out.astype(BF16)
    x2 = x.reshape(B * S, D)
    for l in range(depth):
        lam_init = 0.8 - 0.6 * math.exp(-0.3 * l)
        mod3 = mod[l].reshape(B, 1, 3 * D)
        z2 = _in_proj(x2, mod3, g_pre[l], w_in_bf, cos, sin, seq=S, layer=l)
        diff_out = _diff_attention(z2.reshape(B, S, D_IN), lambda_q1[l], lambda_k1[l],
                                   lambda_q2[l], lambda_k2[l], subln_g[l], lam_init=lam_init)
        x2 = _out_proj(z2, diff_out.reshape(B * S, D_DIFF), w_pool_bf, pool_scale[l],
                       w_out_bf, x2, mod3, g_post[l], seq=S, layer=l)
    return x2.reshape(B, S, D)
```

```python
import functools
import math

import jax
import jax.numpy as jnp
from jax import lax
from jax.experimental import pallas as pl
from jax.experimental.pallas import tpu as pltpu
import numpy as np

D_MODEL = 2048
CHUNK = 64
D_POOL = 1024
D_DIFF = 1024
POOL_WINDOWS = (2, 4, 8, 16)
N_POOL_GROUPS = 4
POOL_GROUP_DIM = 256
DIFF_HEADS = 8
DIFF_V_DIM = 128
DIFF_QK_DIM = 64
D_QK = 1024
D_IN = 6144
D_Z = 2 * D_POOL + D_QK + D_DIFF
ROPE_THETA = 10000.0
EPS = 1e-6
NEG_INF = -1e30

LANES = 128
ONES_ROWS = 16
ATTN_TQ = 512
POOL_HALO = 16
VMEM_LIMIT = 56 * 1024 * 1024
LOG2_E = math.log2(math.e)

F32 = jnp.float32
BF16 = jnp.bfloat16


def _silu(g):
    return g * jax.nn.sigmoid(g)


def _rope_table_kernel(pos_ref, freq_ref, sign_ref, cos_ref, sin_ref):
    ang = pos_ref[0].astype(F32) * freq_ref[...]
    cos_ref[0] = jnp.cos(ang)
    sin_ref[0] = jnp.sin(ang) * sign_ref[...]


def _rope_tables(positions):
    B, S = positions.shape
    ts = 512
    inv_freq = 1.0 / (ROPE_THETA ** (jnp.arange(0, DIFF_QK_DIM, 2, dtype=F32) / DIFF_QK_DIM))
    freq = jnp.tile(inv_freq, LANES // (DIFF_QK_DIM // 2))[None, :]
    sign = jnp.where(jnp.arange(LANES) < LANES // 2, -1.0, 1.0).astype(F32)[None, :]
    cos, sin = pl.pallas_call(
        _rope_table_kernel,
        grid=(B, S // ts),
        in_specs=[
            pl.BlockSpec((1, ts, 1), lambda b, i: (b, i, 0)),
            pl.BlockSpec((1, LANES), lambda b, i: (0, 0)),
            pl.BlockSpec((1, LANES), lambda b, i: (0, 0)),
        ],
        out_specs=[
            pl.BlockSpec((1, ts, LANES), lambda b, i: (b, i, 0)),
            pl.BlockSpec((1, ts, LANES), lambda b, i: (b, i, 0)),
        ],
        out_shape=[jax.ShapeDtypeStruct((B, S, LANES), F32)] * 2,
        name="rope_tables",
    )(positions.reshape(B, S, 1), freq, sign)
    return cos.reshape(B * S, LANES), sin.reshape(B * S, LANES)


def _ada_kernel(c_ref, w_ref, b_ref, o_ref):
    o_ref[0] = jnp.dot(c_ref[...], w_ref[0], preferred_element_type=F32) + b_ref[0]


def _ada_modulation(c, w_ada, b_ada):
    depth, d, n = w_ada.shape
    B = c.shape[0]
    rows = 8
    c_pad = jnp.zeros((rows, d), F32).at[:B].set(c)
    tn = 1024
    out = pl.pallas_call(
        _ada_kernel,
        grid=(depth, n // tn),
        in_specs=[
            pl.BlockSpec((rows, d), lambda l, j: (0, 0)),
            pl.BlockSpec((1, d, tn), lambda l, j: (l, 0, j)),
            pl.BlockSpec((1, 1, tn), lambda l, j: (l, 0, j)),
        ],
        out_specs=pl.BlockSpec((1, rows, tn), lambda l, j: (l, 0, j)),
        out_shape=jax.ShapeDtypeStruct((depth, rows, n), F32),
        compiler_params=pltpu.CompilerParams(vmem_limit_bytes=VMEM_LIMIT),
        name="ada_modulation",
    )(c_pad, w_ada, b_ada.reshape(depth, 1, n))
    return out[:, :B]


def _in_proj_kernel(x0_ref, xn_ref, shift0_ref, scale0_ref, shiftn_ref, scalen_ref, gpre_ref,
                    w_ref, cos_ref, sin_ref, z_ref, qt_ref, vt_ref, h_ref, *, tn):
    i = pl.program_id(0)
    n_col_tiles = w_ref.shape[1] // tn
    q_tile = (2 * D_POOL) // tn
    k_tile = (2 * D_POOL + D_QK) // tn
    v_tile = (2 * D_POOL + 2 * D_QK) // tn

    def normed(x_ref, shift_ref, scale_ref):
        x = x_ref[...]
        ms = jnp.mean(x * x, axis=-1, keepdims=True)
        y = x * lax.rsqrt(ms + EPS) * gpre_ref[...]
        return (y * (1.0 + scale_ref[0]) + shift_ref[0]).astype(BF16)

    @pl.when(i == 0)
    def _():
        h_ref[0] = normed(x0_ref, shift0_ref, scale0_ref)

    def step(cur):
        h_ref[1 - cur] = normed(xn_ref, shiftn_ref, scalen_ref)
        h = h_ref[cur]
        z_tile = 0
        for jt in range(n_col_tiles):
            acc = jnp.dot(h, w_ref[:, jt * tn:(jt + 1) * tn], preferred_element_type=F32)
            if jt in (q_tile, k_tile):
                qk_scale = LOG2_E / math.sqrt(DIFF_QK_DIM) if jt == q_tile else 1.0
                cos = cos_ref[...] * qk_scale
                sin = sin_ref[...] * qk_scale
                for g in range(tn // LANES):
                    xg = acc[:, g * LANES:(g + 1) * LANES]
                    rot = pltpu.roll(xg, LANES // 2, 1)
                    roped = xg * cos + rot * sin
                    if jt == q_tile:
                        qt_ref[g] = roped.T.astype(BF16)
                    else:
                        col = z_tile * tn + g * LANES
                        z_ref[:, col:col + LANES] = roped.astype(BF16)
            elif jt == v_tile:
                for g in range(tn // LANES):
                    vt_ref[g] = acc[:, g * LANES:(g + 1) * LANES].T.astype(BF16)
            else:
                z_ref[:, z_tile * tn:(z_tile + 1) * tn] = acc.astype(BF16)
            if jt not in (q_tile, v_tile):
                z_tile += 1

    for parity in range(2):
        pl.when(i % 2 == parity)(functools.partial(step, parity))


def _in_proj(x2, mod3, g_pre, w_in_bf, cos, sin, *, seq, layer):
    M, D = x2.shape
    N = w_in_bf.shape[2]
    tm, tn = 256, 1024
    per_seq = seq // tm
    last = M // tm - 1
    nxt = lambda i: jnp.minimum(i + 1, last)
    kern = functools.partial(_in_proj_kernel, tn=tn)
    return pl.pallas_call(
        kern,
        grid=(M // tm,),
        in_specs=[
            pl.BlockSpec((tm, D), lambda i: (0, 0)),
            pl.BlockSpec((tm, D), lambda i: (nxt(i), 0)),
            pl.BlockSpec((1, 1, D), lambda i: (0, 0, 0)),
            pl.BlockSpec((1, 1, D), lambda i: (0, 0, 1)),
            pl.BlockSpec((1, 1, D), lambda i: (nxt(i) // per_seq, 0, 0)),
            pl.BlockSpec((1, 1, D), lambda i: (nxt(i) // per_seq, 0, 1)),
            pl.BlockSpec((1, D), lambda i: (0, 0)),
            pl.BlockSpec((None, D, N), lambda i: (layer, 0, 0), pipeline_mode=pl.Buffered(1)),
            pl.BlockSpec((tm, LANES), lambda i: (i, 0)),
            pl.BlockSpec((tm, LANES), lambda i: (i, 0)),
        ],
        out_specs=[
            pl.BlockSpec((tm, D_Z), lambda i: (i, 0)),
            pl.BlockSpec((None, DIFF_HEADS, LANES, tm), lambda i: (i // per_seq, 0, 0, i % per_seq)),
            pl.BlockSpec((None, DIFF_HEADS, LANES, tm), lambda i: (i // per_seq, 0, 0, i % per_seq)),
        ],
        out_shape=[
            jax.ShapeDtypeStruct((M, D_Z), BF16),
            jax.ShapeDtypeStruct((M // seq, DIFF_HEADS, LANES, seq), BF16),
            jax.ShapeDtypeStruct((M // seq, DIFF_HEADS, LANES, seq), BF16),
        ],
        scratch_shapes=[pltpu.VMEM((2, tm, D), BF16)],
        compiler_params=pltpu.CompilerParams(
            dimension_semantics=("arbitrary",), vmem_limit_bytes=VMEM_LIMIT),
        name="in_proj",
    )(x2, x2, mod3, mod3, mod3, mod3, g_pre.reshape(1, D), w_in_bf, cos, sin)


def _attn_kernel(lq1_ref, lk1_ref, lq2_ref, lk2_ref, subg_ref, q_ref, k_ref, v_ref, g_ref,
                 o_ref, vt_ref, qcat_ref, acc_ref, *, seq, tq, lam_init):
    tk, hq, nq = tq, tq // 2, seq // tq
    lam = (jnp.exp(jnp.sum(lq1_ref[...] * lk1_ref[...], axis=-1, keepdims=True))
           - jnp.exp(jnp.sum(lq2_ref[...] * lk2_ref[...], axis=-1, keepdims=True))
           + lam_init)

    dim = lax.broadcasted_iota(jnp.int32, (LANES, tq), 0)
    first_map = (dim // (DIFF_QK_DIM // 2)) % 2 == 0
    for c in range(nq):
        cols = slice(c * tq, (c + 1) * tq)
        vt_ref[c, :DIFF_V_DIM, :] = v_ref[:, cols]
        vt_ref[c, DIFF_V_DIM:, :] = jnp.ones((ONES_ROWS, tk), BF16)
        q = q_ref[:, cols]
        zero = jnp.zeros_like(q)
        q0, q1 = jnp.where(first_map, q, zero), jnp.where(first_map, zero, q)
        qcat_ref[c] = jnp.concatenate([q0[:, :hq], q1[:, :hq], q0[:, hq:], q1[:, hq:]], axis=1)

    key_a = lax.broadcasted_iota(jnp.int32, (hq, 2 * tq), 0)
    col_a = lax.broadcasted_iota(jnp.int32, (hq, 2 * tq), 1)
    mask_a = key_a // CHUNK <= (col_a % hq + hq * (col_a // tq)) // CHUNK
    key_b = lax.broadcasted_iota(jnp.int32, (hq, tq), 0)
    col_b = lax.broadcasted_iota(jnp.int32, (hq, tq), 1)
    mask_b = key_b // CHUNK <= (col_b % hq) // CHUNK

    def scores(tile):
        kind, i, j = tile
        if kind == "full":
            keys, qc = k_ref[j * tk:(j + 1) * tk, :], qcat_ref[i]
        elif kind == "diag_a":
            keys, qc = k_ref[i * tk:i * tk + hq, :], qcat_ref[i]
        else:
            keys, qc = k_ref[i * tk + hq:(i + 1) * tk, :], qcat_ref[i, :, tq:]
        return jnp.dot(keys, qc, preferred_element_type=F32)

    def softmax_pv(tile, s, m):
        kind, i, j = tile
        acc = acc_ref.at[i % 2]
        if kind == "full":
            vt, cols = vt_ref[j], slice(None)
        elif kind == "diag_a":
            s = jnp.where(mask_a, s, NEG_INF)
            vt, cols = vt_ref[i, :, :hq], slice(None)
        else:
            s = jnp.where(mask_b, s, NEG_INF)
            vt, cols, m = vt_ref[i, :, hq:], slice(tq, None), m[:, tq:]
        m_new = jnp.max(s, axis=0, keepdims=True)
        if m is not None:
            m_new = jnp.maximum(m, m_new)
        p = jnp.exp2(s - m_new).astype(BF16)
        pv = jnp.dot(vt, p, preferred_element_type=F32)
        if m is None:
            acc[...] = pv
        else:
            acc[:, cols] = acc[:, cols] * jnp.exp2(m - m_new) + pv
        return m_new

    def finish(i):
        acc = acc_ref.at[i % 2]
        o = acc[:DIFF_V_DIM, :] / acc[DIFF_V_DIM:DIFF_V_DIM + 1, :]
        o0 = jnp.concatenate([o[:, :hq], o[:, tq:tq + hq]], axis=1)
        o1 = jnp.concatenate([o[:, hq:tq], o[:, tq + hq:]], axis=1)
        att = (o0 - lam * o1).T
        ms = jnp.mean(att * att, axis=-1, keepdims=True)
        y = att * lax.rsqrt(ms + EPS) * subg_ref[...] * (1.0 - lam_init)
        rows = slice(i * tq, (i + 1) * tq)
        o_ref[rows, :] = (y * _silu(g_ref[rows, :].astype(F32))).astype(BF16)

    tiles = []
    for i in range(nq):
        tiles += [("full", i, j) for j in range(i)] + [("diag_a", i, i), ("diag_b", i, i)]
    s_next = scores(tiles[0])
    m = None
    for n, tile in enumerate(tiles):
        s = s_next
        if n + 1 < len(tiles):
            s_next = scores(tiles[n + 1])
        m = softmax_pv(tile, s, m)
        if tile[0] == "diag_b":
            finish(tile[1])
            m = None


def _diff_attention(z3, q_t, v_t, lq1, lk1, lq2, lk2, subln_g, *, lam_init):
    B, S, _ = z3.shape
    tq = ATTN_TQ
    k_col = (2 * D_POOL) // LANES
    g_col = k_col + D_QK // LANES
    vec = lambda n: pl.BlockSpec((1, n), lambda b, h: (0, 0))
    head = lambda col: pl.BlockSpec((None, S, LANES), lambda b, h: (b, 0, col + h))
    head_t = pl.BlockSpec((None, None, LANES, S), lambda b, h: (b, h, 0, 0))
    kern = functools.partial(_attn_kernel, seq=S, tq=tq, lam_init=lam_init)
    return pl.pallas_call(
        kern,
        grid=(B, DIFF_HEADS),
        in_specs=[vec(DIFF_QK_DIM)] * 4 + [vec(DIFF_V_DIM),
                  head_t, head(k_col), head_t, head(g_col)],
        out_specs=pl.BlockSpec((None, S, LANES), lambda b, h: (b, 0, h)),
        out_shape=jax.ShapeDtypeStruct((B, S, D_DIFF), BF16),
        scratch_shapes=[pltpu.VMEM((S // tq, DIFF_V_DIM + ONES_ROWS, tq), BF16),
                        pltpu.VMEM((S // tq, LANES, 2 * tq), BF16),
                        pltpu.VMEM((2, DIFF_V_DIM + ONES_ROWS, 2 * tq), F32)],
        compiler_params=pltpu.CompilerParams(
            dimension_semantics=("parallel", "parallel"), vmem_limit_bytes=VMEM_LIMIT),
        name="diff_attention",
    )(lq1.reshape(1, -1), lk1.reshape(1, -1), lq2.reshape(1, -1), lk2.reshape(1, -1),
      subln_g.reshape(1, -1), q_t, z3, v_t, z3)


def _pool_bands(tm):
    lag = np.arange(tm)[:, None] - np.arange(tm)[None, :]
    hlag = np.arange(POOL_HALO)[:, None] + POOL_HALO - np.arange(POOL_HALO)[None, :]
    band = np.stack([(lag >= 0) & (lag < w) for w in POOL_WINDOWS]).astype(np.float32)
    hband = np.stack([hlag < w for w in POOL_WINDOWS]).astype(np.float32)
    return jnp.asarray(band, BF16), jnp.asarray(hband, BF16)


def _out_proj_kernel(u_ref, halo_ref, gp_ref, d_ref, band_ref, hband_ref, wp_ref, ps_ref,
                     w_ref, x_ref, gate_ref, gpost_ref, o_ref, pool_ref, *, tm, per_seq):
    it = pl.program_id(0) % per_seq
    sub = band_ref.shape[1]
    gd = POOL_GROUP_DIM
    for r in range(tm // sub):
        rows = slice(r * sub, (r + 1) * sub)
        t1 = it * tm + r * sub + lax.broadcasted_iota(jnp.int32, (sub, 1), 0) + 1
        y = jnp.dot(d_ref[rows, :], w_ref[D_POOL:, :], preferred_element_type=F32)
        for g, window in enumerate(POOL_WINDOWS):
            cols = slice(g * gd, (g + 1) * gd)
            u = u_ref[rows, cols]
            win = jnp.dot(band_ref[g], u, preferred_element_type=F32)
            if r == 0:
                hwin = jnp.dot(hband_ref[g], halo_ref[:, cols], preferred_element_type=F32)
                hwin = jnp.where(it == 0, 0.0, hwin)
            else:
                hwin = jnp.dot(hband_ref[g], u_ref[r * sub - POOL_HALO:r * sub, cols],
                               preferred_element_type=F32)
            win = jnp.concatenate([win[:POOL_HALO] + hwin, win[POOL_HALO:]], axis=0)
            count = jnp.minimum(t1, window).astype(F32)
            pooled = win / count - u.astype(F32)
            mixed = (jnp.dot(pooled.astype(BF16), wp_ref[g], preferred_element_type=F32)
                     * ps_ref[:, cols])
            pool_ref[rows, cols] = (mixed * _silu(gp_ref[rows, cols].astype(F32))).astype(BF16)

        y = y + jnp.dot(pool_ref[rows, :], w_ref[:D_POOL, :], preferred_element_type=F32)
        ms = jnp.mean(y * y, axis=-1, keepdims=True)
        yn = y * lax.rsqrt(ms + EPS) * gpost_ref[...]
        o_ref[rows, :] = x_ref[rows, :] + (1.0 + gate_ref[0]) * yn


def _out_proj(z2, diff_out, w_pool_bf, pool_scale, w_out_bf, x2, mod3, g_post, *, seq, layer):
    M, D = x2.shape
    tm = 512
    per_seq = seq // tm
    band, hband = _pool_bands(tm // 2)
    const = lambda shape: pl.BlockSpec(shape, lambda i: (0,) * len(shape))
    of_layer = lambda shape: pl.BlockSpec((None,) + shape, lambda i: (layer,) + (0,) * len(shape))
    kern = functools.partial(_out_proj_kernel, tm=tm, per_seq=per_seq)
    return pl.pallas_call(
        kern,
        grid=(M // tm,),
        in_specs=[
            pl.BlockSpec((tm, D_POOL), lambda i: (i, 0)),
            pl.BlockSpec((POOL_HALO, D_POOL),
                         lambda i: (jnp.maximum(i * (tm // POOL_HALO) - 1, 0), 0)),
            pl.BlockSpec((tm, D_POOL), lambda i: (i, 1)),
            pl.BlockSpec((tm, D_DIFF), lambda i: (i, 0)),
            const(band.shape), const(hband.shape), of_layer(w_pool_bf.shape[1:]),
            const((1, D_POOL)), of_layer((D, D)),
            pl.BlockSpec((tm, D), lambda i: (i, 0)),
            pl.BlockSpec((1, 1, D), lambda i: (i // per_seq, 0, 2)),
            const((1, D)),
        ],
        out_specs=pl.BlockSpec((tm, D), lambda i: (i, 0)),
        out_shape=jax.ShapeDtypeStruct((M, D), F32),
        scratch_shapes=[pltpu.VMEM((tm, D_POOL), BF16)],
        compiler_params=pltpu.CompilerParams(
            dimension_semantics=("parallel",), vmem_limit_bytes=VMEM_LIMIT),
        name="out_proj",
    )(z2, z2, z2, diff_out, band, hband, w_pool_bf, pool_scale.reshape(1, D_POOL),
      w_out_bf, x2, mod3, g_post.reshape(1, D))


def _w_in_prep_kernel(w_ref, o_ref, *, tn):
    j = pl.program_id(1)
    first_qk = (2 * D_POOL) // tn
    is_qk = jnp.logical_and(j >= first_qk, j < first_qk + (2 * D_QK) // tn)

    @pl.when(is_qk)
    def _():
        quarter = DIFF_QK_DIM // 2
        blk = lax.broadcasted_iota(jnp.int32, (w_ref.shape[0], LANES), 1) // quarter
        for g in range(tn // LANES):
            x = w_ref[:, g * LANES:(g + 1) * LANES]
            y = jnp.where(blk == 1, pltpu.roll(x, LANES - quarter, 1),
                          jnp.where(blk == 2, pltpu.roll(x, quarter, 1), x))
            o_ref[:, g * LANES:(g + 1) * LANES] = y.astype(BF16)

    @pl.when(jnp.logical_not(is_qk))
    def _():
        o_ref[...] = w_ref[...].astype(BF16)


def _w_in_prep(w_in):
    depth, d, n = w_in.shape
    tn = 512
    return pl.pallas_call(
        functools.partial(_w_in_prep_kernel, tn=tn),
        grid=(depth, n // tn),
        in_specs=[pl.BlockSpec((None, d, tn), lambda l, j: (l, 0, j))],
        out_specs=pl.BlockSpec((None, d, tn), lambda l, j: (l, 0, j)),
        out_shape=jax.ShapeDtypeStruct((depth, d, n), BF16),
        compiler_params=pltpu.CompilerParams(
            dimension_semantics=("parallel", "parallel"), vmem_limit_bytes=VMEM_LIMIT),
        name="w_in_prep",
    )(w_in)


def kernel(x, c, positions, w_ada, b_ada, g_pre, w_in, w_pool, pool_scale, lambda_q1, lambda_k1,
           lambda_q2, lambda_k2, subln_g, w_out, g_post):
    B, S, D = x.shape
    depth = w_in.shape[0]
    cos, sin = _rope_tables(positions)
    mod = _ada_modulation(c, w_ada, b_ada)
    w_in_bf = _w_in_prep(w_in)
    w_pool_bf = w_pool.astype(BF16)
    w_out_bf = w_out.astype(BF16)
    x2 = x.reshape(B * S, D)
    for l in range(depth):
        lam_init = 0.8 - 0.6 * math.exp(-0.3 * l)
        mod3 = mod[l].reshape(B, 1, 3 * D)
        z2, q_t, v_t = _in_proj(x2, mod3, g_pre[l], w_in_bf, cos, sin, seq=S, layer=l)
        diff_out = _diff_attention(z2.reshape(B, S, D_Z), q_t, v_t, lambda_q1[l], lambda_k1[l],
                                   lambda_q2[l], lambda_k2[l], subln_g[l], lam_init=lam_init)
        x2 = _out_proj(z2, diff_out.reshape(B * S, D_DIFF), w_pool_bf, pool_scale[l],
                       w_out_bf, x2, mod3, g_post[l], seq=S, layer=l)
    return x2.reshape(B, S, D)
```

```python
import functools
import math

import jax
import jax.numpy as jnp
from jax import lax
from jax.experimental import pallas as pl
from jax.experimental.pallas import tpu as pltpu
import numpy as np

D_MODEL = 2048
CHUNK = 64
D_POOL = 1024
D_DIFF = 1024
POOL_WINDOWS = (2, 4, 8, 16)
N_POOL_GROUPS = 4
POOL_GROUP_DIM = 256
DIFF_HEADS = 8
DIFF_V_DIM = 128
DIFF_QK_DIM = 64
D_QK = 1024
D_IN = 6144
D_Z = 2 * D_POOL + D_QK + D_DIFF
ROPE_THETA = 10000.0
EPS = 1e-6
NEG_INF = -1e30

LANES = 128
ONES_ROWS = 16
ATTN_TQ = 512
SCORE_LOOKAHEAD = 3
POOL_HALO = 16
VMEM_LIMIT = 56 * 1024 * 1024
LOG2_E = math.log2(math.e)

F32 = jnp.float32
BF16 = jnp.bfloat16


def _silu(g):
    return g * jax.nn.sigmoid(g)


def _rope_table_kernel(pos_ref, freq_ref, sign_ref, cos_ref, sin_ref):
    ang = pos_ref[0].astype(F32) * freq_ref[...]
    cos_ref[0] = jnp.cos(ang)
    sin_ref[0] = jnp.sin(ang) * sign_ref[...]


def _rope_tables(positions):
    B, S = positions.shape
    ts = 512
    inv_freq = 1.0 / (ROPE_THETA ** (jnp.arange(0, DIFF_QK_DIM, 2, dtype=F32) / DIFF_QK_DIM))
    freq = jnp.tile(inv_freq, LANES // (DIFF_QK_DIM // 2))[None, :]
    sign = jnp.where(jnp.arange(LANES) < LANES // 2, -1.0, 1.0).astype(F32)[None, :]
    cos, sin = pl.pallas_call(
        _rope_table_kernel,
        grid=(B, S // ts),
        in_specs=[
            pl.BlockSpec((1, ts, 1), lambda b, i: (b, i, 0)),
            pl.BlockSpec((1, LANES), lambda b, i: (0, 0)),
            pl.BlockSpec((1, LANES), lambda b, i: (0, 0)),
        ],
        out_specs=[
            pl.BlockSpec((1, ts, LANES), lambda b, i: (b, i, 0)),
            pl.BlockSpec((1, ts, LANES), lambda b, i: (b, i, 0)),
        ],
        out_shape=[jax.ShapeDtypeStruct((B, S, LANES), F32)] * 2,
        name="rope_tables",
    )(positions.reshape(B, S, 1), freq, sign)
    return cos.reshape(B * S, LANES), sin.reshape(B * S, LANES)


def _ada_kernel(c_ref, w_ref, b_ref, o_ref):
    o_ref[0] = jnp.dot(c_ref[...], w_ref[0], preferred_element_type=F32) + b_ref[0]


def _ada_modulation(c, w_ada, b_ada):
    depth, d, n = w_ada.shape
    B = c.shape[0]
    rows = 8
    c_pad = jnp.zeros((rows, d), F32).at[:B].set(c)
    tn = 1024
    out = pl.pallas_call(
        _ada_kernel,
        grid=(depth, n // tn),
        in_specs=[
            pl.BlockSpec((rows, d), lambda l, j: (0, 0)),
            pl.BlockSpec((1, d, tn), lambda l, j: (l, 0, j)),
            pl.BlockSpec((1, 1, tn), lambda l, j: (l, 0, j)),
        ],
        out_specs=pl.BlockSpec((1, rows, tn), lambda l, j: (l, 0, j)),
        out_shape=jax.ShapeDtypeStruct((depth, rows, n), F32),
        compiler_params=pltpu.CompilerParams(vmem_limit_bytes=VMEM_LIMIT),
        name="ada_modulation",
    )(c_pad, w_ada, b_ada.reshape(depth, 1, n))
    return out[:, :B]


def _in_proj_kernel(x0_ref, xn_ref, shift0_ref, scale0_ref, shiftn_ref, scalen_ref, gpre_ref,
                    w_ref, cos_ref, sin_ref, z_ref, qt_ref, vt_ref, h_ref, *, tn):
    i = pl.program_id(0)
    n_col_tiles = w_ref.shape[1] // tn
    q_tile = (2 * D_POOL) // tn
    k_tile = (2 * D_POOL + D_QK) // tn
    v_tile = (2 * D_POOL + 2 * D_QK) // tn

    def normed(x_ref, shift_ref, scale_ref):
        x = x_ref[...]
        ms = jnp.mean(x * x, axis=-1, keepdims=True)
        y = x * lax.rsqrt(ms + EPS) * gpre_ref[...]
        return (y * (1.0 + scale_ref[0]) + shift_ref[0]).astype(BF16)

    @pl.when(i == 0)
    def _():
        h_ref[0] = normed(x0_ref, shift0_ref, scale0_ref)

    def step(cur):
        h_ref[1 - cur] = normed(xn_ref, shiftn_ref, scalen_ref)
        h = h_ref[cur]
        z_tile = 0
        for jt in range(n_col_tiles):
            acc = jnp.dot(h, w_ref[:, jt * tn:(jt + 1) * tn], preferred_element_type=F32)
            if jt in (q_tile, k_tile):
                qk_scale = LOG2_E / math.sqrt(DIFF_QK_DIM) if jt == q_tile else 1.0
                cos = cos_ref[...] * qk_scale
                sin = sin_ref[...] * qk_scale
                for g in range(tn // LANES):
                    xg = acc[:, g * LANES:(g + 1) * LANES]
                    rot = pltpu.roll(xg, LANES // 2, 1)
                    roped = xg * cos + rot * sin
                    if jt == q_tile:
                        qt_ref[g] = roped.T.astype(BF16)
                    else:
                        col = z_tile * tn + g * LANES
                        z_ref[:, col:col + LANES] = roped.astype(BF16)
            elif jt == v_tile:
                for g in range(tn // LANES):
                    vt_ref[g] = acc[:, g * LANES:(g + 1) * LANES].T.astype(BF16)
            else:
                z_ref[:, z_tile * tn:(z_tile + 1) * tn] = acc.astype(BF16)
            if jt not in (q_tile, v_tile):
                z_tile += 1

    for parity in range(2):
        pl.when(i % 2 == parity)(functools.partial(step, parity))


def _in_proj(x2, mod3, g_pre, w_in_bf, cos, sin, *, seq, layer):
    M, D = x2.shape
    N = w_in_bf.shape[2]
    tm, tn = 256, 1024
    per_seq = seq // tm
    last = M // tm - 1
    nxt = lambda i: jnp.minimum(i + 1, last)
    kern = functools.partial(_in_proj_kernel, tn=tn)
    return pl.pallas_call(
        kern,
        grid=(M // tm,),
        in_specs=[
            pl.BlockSpec((tm, D), lambda i: (0, 0)),
            pl.BlockSpec((tm, D), lambda i: (nxt(i), 0)),
            pl.BlockSpec((1, 1, D), lambda i: (0, 0, 0)),
            pl.BlockSpec((1, 1, D), lambda i: (0, 0, 1)),
            pl.BlockSpec((1, 1, D), lambda i: (nxt(i) // per_seq, 0, 0)),
            pl.BlockSpec((1, 1, D), lambda i: (nxt(i) // per_seq, 0, 1)),
            pl.BlockSpec((1, D), lambda i: (0, 0)),
            pl.BlockSpec((None, D, N), lambda i: (layer, 0, 0), pipeline_mode=pl.Buffered(1)),
            pl.BlockSpec((tm, LANES), lambda i: (i, 0)),
            pl.BlockSpec((tm, LANES), lambda i: (i, 0)),
        ],
        out_specs=[
            pl.BlockSpec((tm, D_Z), lambda i: (i, 0)),
            pl.BlockSpec((None, DIFF_HEADS, LANES, tm), lambda i: (i // per_seq, 0, 0, i % per_seq)),
            pl.BlockSpec((None, DIFF_HEADS, LANES, tm), lambda i: (i // per_seq, 0, 0, i % per_seq)),
        ],
        out_shape=[
            jax.ShapeDtypeStruct((M, D_Z), BF16),
            jax.ShapeDtypeStruct((M // seq, DIFF_HEADS, LANES, seq), BF16),
            jax.ShapeDtypeStruct((M // seq, DIFF_HEADS, LANES, seq), BF16),
        ],
        scratch_shapes=[pltpu.VMEM((2, tm, D), BF16)],
        compiler_params=pltpu.CompilerParams(
            dimension_semantics=("arbitrary",), vmem_limit_bytes=VMEM_LIMIT),
        name="in_proj",
    )(x2, x2, mod3, mod3, mod3, mod3, g_pre.reshape(1, D), w_in_bf, cos, sin)


def _attn_kernel(lq1_ref, lk1_ref, lq2_ref, lk2_ref, subg_ref, q_ref, k_ref, v_ref, g_ref,
                 o_ref, vt_ref, qcat_ref, acc_ref, *, seq, tq, lam_init):
    tk, hq, nq = tq, tq // 2, seq // tq
    lam = (jnp.exp(jnp.sum(lq1_ref[...] * lk1_ref[...], axis=-1, keepdims=True))
           - jnp.exp(jnp.sum(lq2_ref[...] * lk2_ref[...], axis=-1, keepdims=True))
           + lam_init)

    dim = lax.broadcasted_iota(jnp.int32, (LANES, tq), 0)
    first_map = (dim // (DIFF_QK_DIM // 2)) % 2 == 0
    for c in range(nq):
        cols = slice(c * tq, (c + 1) * tq)
        vt_ref[c, :DIFF_V_DIM, :] = v_ref[:, cols]
        vt_ref[c, DIFF_V_DIM:, :] = jnp.ones((ONES_ROWS, tk), BF16)
        q = q_ref[:, cols]
        zero = jnp.zeros_like(q)
        q0, q1 = jnp.where(first_map, q, zero), jnp.where(first_map, zero, q)
        qcat_ref[c] = jnp.concatenate([q0[:, :hq], q1[:, :hq], q0[:, hq:], q1[:, hq:]], axis=1)

    n_groups = 2 * tq // hq
    key_c = lax.broadcasted_iota(jnp.int32, (hq, hq), 0) // CHUNK
    row_c = lax.broadcasted_iota(jnp.int32, (hq, hq), 1) // CHUNK
    causal = key_c <= row_c

    def scores(tile):
        kind, i, j, c = tile
        qc = qcat_ref[i, :, c * hq:(c + 1) * hq]
        if kind == "full":
            keys = k_ref[j * tk:(j + 1) * tk, :]
        elif kind == "diag_a":
            keys = k_ref[i * tk:i * tk + hq, :]
        else:
            keys = k_ref[i * tk + hq:(i + 1) * tk, :]
        return jnp.dot(keys, qc, preferred_element_type=F32)

    def softmax_pv(tile, s, m):
        kind, i, j, c = tile
        acc = acc_ref.at[i % 2]
        cols = slice(c * hq, (c + 1) * hq)
        if kind == "full":
            vt = vt_ref[j]
        elif kind == "diag_a":
            vt = vt_ref[i, :, :hq]
            if c < n_groups // 2:
                s = jnp.where(causal, s, NEG_INF)
        else:
            vt = vt_ref[i, :, hq:]
            s = jnp.where(causal, s, NEG_INF)
        m_new = jnp.max(s, axis=0, keepdims=True)
        if m is not None:
            m_new = jnp.maximum(m, m_new)
        p = jnp.exp2(s - m_new).astype(BF16)
        pv = jnp.dot(vt, p, preferred_element_type=F32)
        if m is None:
            acc[:, cols] = pv
        else:
            acc[:, cols] = acc[:, cols] * jnp.exp2(m - m_new) + pv
        return m_new

    def finish(i):
        acc = acc_ref.at[i % 2]
        o = acc[:DIFF_V_DIM, :] / acc[DIFF_V_DIM:DIFF_V_DIM + 1, :]
        o0 = jnp.concatenate([o[:, :hq], o[:, tq:tq + hq]], axis=1)
        o1 = jnp.concatenate([o[:, hq:tq], o[:, tq + hq:]], axis=1)
        att = (o0 - lam * o1).T
        ms = jnp.mean(att * att, axis=-1, keepdims=True)
        y = att * lax.rsqrt(ms + EPS) * subg_ref[...] * (1.0 - lam_init)
        rows = slice(i * tq, (i + 1) * tq)
        o_ref[rows, :] = (y * _silu(g_ref[rows, :].astype(F32))).astype(BF16)

    tiles = []
    for i in range(nq):
        for j in range(i):
            tiles += [("full", i, j, c) for c in range(n_groups)]
        tiles += [("diag_a", i, i, c) for c in range(n_groups)]
        tiles += [("diag_b", i, i, c) for c in range(n_groups // 2, n_groups)]
    pending = [scores(t) for t in tiles[:SCORE_LOOKAHEAD]]
    m = {}
    for n, tile in enumerate(tiles):
        if n + SCORE_LOOKAHEAD < len(tiles):
            pending.append(scores(tiles[n + SCORE_LOOKAHEAD]))
        kind, i, _, c = tile
        m[c] = softmax_pv(tile, pending.pop(0), m.get(c))
        if kind == "diag_b" and c == n_groups - 1:
            finish(i)
            m = {}


def _diff_attention(z3, q_t, v_t, lq1, lk1, lq2, lk2, subln_g, *, lam_init):
    B, S, _ = z3.shape
    tq = ATTN_TQ
    k_col = (2 * D_POOL) // LANES
    g_col = k_col + D_QK // LANES
    vec = lambda n: pl.BlockSpec((1, n), lambda b, h: (0, 0))
    head = lambda col: pl.BlockSpec((None, S, LANES), lambda b, h: (b, 0, col + h))
    head_t = pl.BlockSpec((None, None, LANES, S), lambda b, h: (b, h, 0, 0))
    kern = functools.partial(_attn_kernel, seq=S, tq=tq, lam_init=lam_init)
    return pl.pallas_call(
        kern,
        grid=(B, DIFF_HEADS),
        in_specs=[vec(DIFF_QK_DIM)] * 4 + [vec(DIFF_V_DIM),
                  head_t, head(k_col), head_t, head(g_col)],
        out_specs=pl.BlockSpec((None, S, LANES), lambda b, h: (b, 0, h)),
        out_shape=jax.ShapeDtypeStruct((B, S, D_DIFF), BF16),
        scratch_shapes=[pltpu.VMEM((S // tq, DIFF_V_DIM + ONES_ROWS, tq), BF16),
                        pltpu.VMEM((S // tq, LANES, 2 * tq), BF16),
                        pltpu.VMEM((2, DIFF_V_DIM + ONES_ROWS, 2 * tq), F32)],
        compiler_params=pltpu.CompilerParams(
            dimension_semantics=("parallel", "parallel"), vmem_limit_bytes=VMEM_LIMIT),
        name="diff_attention",
    )(lq1.reshape(1, -1), lk1.reshape(1, -1), lq2.reshape(1, -1), lk2.reshape(1, -1),
      subln_g.reshape(1, -1), q_t, z3, v_t, z3)


def _pool_bands(tm):
    lag = np.arange(tm)[:, None] - np.arange(tm)[None, :]
    hlag = np.arange(POOL_HALO)[:, None] + POOL_HALO - np.arange(POOL_HALO)[None, :]
    band = np.stack([(lag >= 0) & (lag < w) for w in POOL_WINDOWS]).astype(np.float32)
    hband = np.stack([hlag < w for w in POOL_WINDOWS]).astype(np.float32)
    return jnp.asarray(band, BF16), jnp.asarray(hband, BF16)


def _out_proj_kernel(u_ref, halo_ref, gp_ref, d_ref, band_ref, hband_ref, wp_ref, ps_ref,
                     w_ref, x_ref, gate_ref, gpost_ref, o_ref, pool_ref, *, tm, per_seq):
    it = pl.program_id(0) % per_seq
    sub = band_ref.shape[1]
    gd = POOL_GROUP_DIM
    for r in range(tm // sub):
        rows = slice(r * sub, (r + 1) * sub)
        t1 = it * tm + r * sub + lax.broadcasted_iota(jnp.int32, (sub, 1), 0) + 1
        y = jnp.dot(d_ref[rows, :], w_ref[D_POOL:, :], preferred_element_type=F32)
        for g, window in enumerate(POOL_WINDOWS):
            cols = slice(g * gd, (g + 1) * gd)
            u = u_ref[rows, cols]
            win = jnp.dot(band_ref[g], u, preferred_element_type=F32)
            if r == 0:
                hwin = jnp.dot(hband_ref[g], halo_ref[:, cols], preferred_element_type=F32)
                hwin = jnp.where(it == 0, 0.0, hwin)
            else:
                hwin = jnp.dot(hband_ref[g], u_ref[r * sub - POOL_HALO:r * sub, cols],
                               preferred_element_type=F32)
            win = jnp.concatenate([win[:POOL_HALO] + hwin, win[POOL_HALO:]], axis=0)
            count = jnp.minimum(t1, window).astype(F32)
            pooled = win / count - u.astype(F32)
            mixed = (jnp.dot(pooled.astype(BF16), wp_ref[g], preferred_element_type=F32)
                     * ps_ref[:, cols])
            pool_ref[rows, cols] = (mixed * _silu(gp_ref[rows, cols].astype(F32))).astype(BF16)

        y = y + jnp.dot(pool_ref[rows, :], w_ref[:D_POOL, :], preferred_element_type=F32)
        ms = jnp.mean(y * y, axis=-1, keepdims=True)
        yn = y * lax.rsqrt(ms + EPS) * gpost_ref[...]
        o_ref[rows, :] = x_ref[rows, :] + (1.0 + gate_ref[0]) * yn


def _out_proj(z2, diff_out, w_pool_bf, pool_scale, w_out_bf, x2, mod3, g_post, *, seq, layer):
    M, D = x2.shape
    tm = 512
    per_seq = seq // tm
    band, hband = _pool_bands(tm // 2)
    const = lambda shape: pl.BlockSpec(shape, lambda i: (0,) * len(shape))
    of_layer = lambda shape: pl.BlockSpec((None,) + shape, lambda i: (layer,) + (0,) * len(shape))
    kern = functools.partial(_out_proj_kernel, tm=tm, per_seq=per_seq)
    return pl.pallas_call(
        kern,
        grid=(M // tm,),
        in_specs=[
            pl.BlockSpec((tm, D_POOL), lambda i: (i, 0)),
            pl.BlockSpec((POOL_HALO, D_POOL),
                         lambda i: (jnp.maximum(i * (tm // POOL_HALO) - 1, 0), 0)),
            pl.BlockSpec((tm, D_POOL), lambda i: (i, 1)),
            pl.BlockSpec((tm, D_DIFF), lambda i: (i, 0)),
            const(band.shape), const(hband.shape), of_layer(w_pool_bf.shape[1:]),
            const((1, D_POOL)), of_layer((D, D)),
            pl.BlockSpec((tm, D), lambda i: (i, 0)),
            pl.BlockSpec((1, 1, D), lambda i: (i // per_seq, 0, 2)),
            const((1, D)),
        ],
        out_specs=pl.BlockSpec((tm, D), lambda i: (i, 0)),
        out_shape=jax.ShapeDtypeStruct((M, D), F32),
        scratch_shapes=[pltpu.VMEM((tm, D_POOL), BF16)],
        compiler_params=pltpu.CompilerParams(
            dimension_semantics=("parallel",), vmem_limit_bytes=VMEM_LIMIT),
        name="out_proj",
    )(z2, z2, z2, diff_out, band, hband, w_pool_bf, pool_scale.reshape(1, D_POOL),
      w_out_bf, x2, mod3, g_post.reshape(1, D))


def _w_in_prep_kernel(w_ref, o_ref, *, tn):
    j = pl.program_id(1)
    first_qk = (2 * D_POOL) // tn
    is_qk = jnp.logical_and(j >= first_qk, j < first_qk + (2 * D_QK) // tn)

    @pl.when(is_qk)
    def _():
        quarter = DIFF_QK_DIM // 2
        blk = lax.broadcasted_iota(jnp.int32, (w_ref.shape[0], LANES), 1) // quarter
        for g in range(tn // LANES):
            x = w_ref[:, g * LANES:(g + 1) * LANES]
            y = jnp.where(blk == 1, pltpu.roll(x, LANES - quarter, 1),
                          jnp.where(blk == 2, pltpu.roll(x, quarter, 1), x))
            o_ref[:, g * LANES:(g + 1) * LANES] = y.astype(BF16)

    @pl.when(jnp.logical_not(is_qk))
    def _():
        o_ref[...] = w_ref[...].astype(BF16)


def _w_in_prep(w_in):
    depth, d, n = w_in.shape
    tn = 512
    return pl.pallas_call(
        functools.partial(_w_in_prep_kernel, tn=tn),
        grid=(depth, n // tn),
        in_specs=[pl.BlockSpec((None, d, tn), lambda l, j: (l, 0, j))],
        out_specs=pl.BlockSpec((None, d, tn), lambda l, j: (l, 0, j)),
        out_shape=jax.ShapeDtypeStruct((depth, d, n), BF16),
        compiler_params=pltpu.CompilerParams(
            dimension_semantics=("parallel", "parallel"), vmem_limit_bytes=VMEM_LIMIT),
        name="w_in_prep",
    )(w_in)


def kernel(x, c, positions, w_ada, b_ada, g_pre, w_in, w_pool, pool_scale, lambda_q1, lambda_k1,
           lambda_q2, lambda_k2, subln_g, w_out, g_post):
    B, S, D = x.shape
    depth = w_in.shape[0]
    cos, sin = _rope_tables(positions)
    mod = _ada_modulation(c, w_ada, b_ada)
    w_in_bf = _w_in_prep(w_in)
    w_pool_bf = w_pool.astype(BF16)
    w_out_bf = w_out.astype(BF16)
    x2 = x.reshape(B * S, D)
    for l in range(depth):
        lam_init = 0.8 - 0.6 * math.exp(-0.3 * l)
        mod3 = mod[l].reshape(B, 1, 3 * D)
        z2, q_t, v_t = _in_proj(x2, mod3, g_pre[l], w_in_bf, cos, sin, seq=S, layer=l)
        diff_out = _diff_attention(z2.reshape(B, S, D_Z), q_t, v_t, lambda_q1[l], lambda_k1[l],
                                   lambda_q2[l], lambda_k2[l], subln_g[l], lam_init=lam_init)
        x2 = _out_proj(z2, diff_out.reshape(B * S, D_DIFF), w_pool_bf, pool_scale[l],
                       w_out_bf, x2, mod3, g_post[l], seq=S, layer=l)
    return x2.reshape(B, S, D)
```

```python
import functools
import math

import jax
import jax.numpy as jnp
from jax import lax
from jax.experimental import pallas as pl
from jax.experimental.pallas import tpu as pltpu
import numpy as np

D_MODEL = 2048
CHUNK = 64
D_POOL = 1024
D_DIFF = 1024
POOL_WINDOWS = (2, 4, 8, 16)
N_POOL_GROUPS = 4
POOL_GROUP_DIM = 256
DIFF_HEADS = 8
DIFF_V_DIM = 128
DIFF_QK_DIM = 64
D_QK = 1024
D_IN = 6144
D_Z = 2 * D_POOL + D_QK + D_DIFF
ROPE_THETA = 10000.0
EPS = 1e-6
NEG_INF = -1e30

LANES = 128
ONES_ROWS = 16
ATTN_TQ = 512
SCORE_LOOKAHEAD = 3
POOL_HALO = 16
VMEM_LIMIT = 56 * 1024 * 1024
LOG2_E = math.log2(math.e)

F32 = jnp.float32
BF16 = jnp.bfloat16


def _silu(g):
    return g * jax.nn.sigmoid(g)


def _rope_table_kernel(pos_ref, freq_ref, sign_ref, cos_ref, sin_ref):
    ang = pos_ref[0].astype(F32) * freq_ref[...]
    cos_ref[0] = jnp.cos(ang)
    sin_ref[0] = jnp.sin(ang) * sign_ref[...]


def _rope_tables(positions):
    B, S = positions.shape
    ts = 512
    inv_freq = 1.0 / (ROPE_THETA ** (jnp.arange(0, DIFF_QK_DIM, 2, dtype=F32) / DIFF_QK_DIM))
    freq = jnp.tile(inv_freq, LANES // (DIFF_QK_DIM // 2))[None, :]
    sign = jnp.where(jnp.arange(LANES) < LANES // 2, -1.0, 1.0).astype(F32)[None, :]
    cos, sin = pl.pallas_call(
        _rope_table_kernel,
        grid=(B, S // ts),
        in_specs=[
            pl.BlockSpec((1, ts, 1), lambda b, i: (b, i, 0)),
            pl.BlockSpec((1, LANES), lambda b, i: (0, 0)),
            pl.BlockSpec((1, LANES), lambda b, i: (0, 0)),
        ],
        out_specs=[
            pl.BlockSpec((1, ts, LANES), lambda b, i: (b, i, 0)),
            pl.BlockSpec((1, ts, LANES), lambda b, i: (b, i, 0)),
        ],
        out_shape=[jax.ShapeDtypeStruct((B, S, LANES), F32)] * 2,
        name="rope_tables",
    )(positions.reshape(B, S, 1), freq, sign)
    return cos.reshape(B * S, LANES), sin.reshape(B * S, LANES)


def _ada_kernel(c_ref, w_ref, b_ref, o_ref):
    o_ref[0] = jnp.dot(c_ref[...], w_ref[0], preferred_element_type=F32) + b_ref[0]


def _ada_modulation(c, w_ada, b_ada):
    depth, d, n = w_ada.shape
    B = c.shape[0]
    rows = 8
    c_pad = jnp.zeros((rows, d), F32).at[:B].set(c)
    tn = 1024
    out = pl.pallas_call(
        _ada_kernel,
        grid=(depth, n // tn),
        in_specs=[
            pl.BlockSpec((rows, d), lambda l, j: (0, 0)),
            pl.BlockSpec((1, d, tn), lambda l, j: (l, 0, j)),
            pl.BlockSpec((1, 1, tn), lambda l, j: (l, 0, j)),
        ],
        out_specs=pl.BlockSpec((1, rows, tn), lambda l, j: (l, 0, j)),
        out_shape=jax.ShapeDtypeStruct((depth, rows, n), F32),
        compiler_params=pltpu.CompilerParams(vmem_limit_bytes=VMEM_LIMIT),
        name="ada_modulation",
    )(c_pad, w_ada, b_ada.reshape(depth, 1, n))
    return out[:, :B]


def _in_proj_kernel(x0_ref, xn_ref, shift0_ref, scale0_ref, shiftn_ref, scalen_ref, gpre_ref,
                    w_ref, cos_ref, sin_ref, z_ref, qt_ref, vt_ref, h_ref, *, tn):
    i = pl.program_id(0)
    n_col_tiles = w_ref.shape[1] // tn
    q_tile = (2 * D_POOL) // tn
    k_tile = (2 * D_POOL + D_QK) // tn
    v_tile = (2 * D_POOL + 2 * D_QK) // tn

    def normed(x_ref, shift_ref, scale_ref):
        x = x_ref[...]
        ms = jnp.mean(x * x, axis=-1, keepdims=True)
        y = x * lax.rsqrt(ms + EPS) * gpre_ref[...]
        return (y * (1.0 + scale_ref[0]) + shift_ref[0]).astype(BF16)

    @pl.when(i == 0)
    def _():
        h_ref[0] = normed(x0_ref, shift0_ref, scale0_ref)

    def step(cur):
        h_ref[1 - cur] = normed(xn_ref, shiftn_ref, scalen_ref)
        h = h_ref[cur]
        z_tile = 0
        for jt in range(n_col_tiles):
            acc = jnp.dot(h, w_ref[:, jt * tn:(jt + 1) * tn], preferred_element_type=F32)
            if jt in (q_tile, k_tile):
                qk_scale = LOG2_E / math.sqrt(DIFF_QK_DIM) if jt == q_tile else 1.0
                cos = cos_ref[...] * qk_scale
                sin = sin_ref[...] * qk_scale
                for g in range(tn // LANES):
                    xg = acc[:, g * LANES:(g + 1) * LANES]
                    rot = pltpu.roll(xg, LANES // 2, 1)
                    roped = xg * cos + rot * sin
                    if jt == q_tile:
                        qt_ref[g] = roped.T.astype(BF16)
                    else:
                        col = z_tile * tn + g * LANES
                        z_ref[:, col:col + LANES] = roped.astype(BF16)
            elif jt == v_tile:
                for g in range(tn // LANES):
                    vt_ref[g] = acc[:, g * LANES:(g + 1) * LANES].T.astype(BF16)
            else:
                z_ref[:, z_tile * tn:(z_tile + 1) * tn] = acc.astype(BF16)
            if jt not in (q_tile, v_tile):
                z_tile += 1

    for parity in range(2):
        pl.when(i % 2 == parity)(functools.partial(step, parity))


def _in_proj(x2, mod3, g_pre, w_in_bf, cos, sin, *, seq, layer):
    M, D = x2.shape
    N = w_in_bf.shape[2]
    tm, tn = 256, 1024
    per_seq = seq // tm
    last = M // tm - 1
    nxt = lambda i: jnp.minimum(i + 1, last)
    kern = functools.partial(_in_proj_kernel, tn=tn)
    return pl.pallas_call(
        kern,
        grid=(M // tm,),
        in_specs=[
            pl.BlockSpec((tm, D), lambda i: (0, 0)),
            pl.BlockSpec((tm, D), lambda i: (nxt(i), 0)),
            pl.BlockSpec((1, 1, D), lambda i: (0, 0, 0)),
            pl.BlockSpec((1, 1, D), lambda i: (0, 0, 1)),
            pl.BlockSpec((1, 1, D), lambda i: (nxt(i) // per_seq, 0, 0)),
            pl.BlockSpec((1, 1, D), lambda i: (nxt(i) // per_seq, 0, 1)),
            pl.BlockSpec((1, D), lambda i: (0, 0)),
            pl.BlockSpec((None, D, N), lambda i: (layer, 0, 0), pipeline_mode=pl.Buffered(1)),
            pl.BlockSpec((tm, LANES), lambda i: (i, 0)),
            pl.BlockSpec((tm, LANES), lambda i: (i, 0)),
        ],
        out_specs=[
            pl.BlockSpec((tm, D_Z), lambda i: (i, 0)),
            pl.BlockSpec((None, DIFF_HEADS, LANES, tm), lambda i: (i // per_seq, 0, 0, i % per_seq)),
            pl.BlockSpec((None, DIFF_HEADS, LANES, tm), lambda i: (i // per_seq, 0, 0, i % per_seq)),
        ],
        out_shape=[
            jax.ShapeDtypeStruct((M, D_Z), BF16),
            jax.ShapeDtypeStruct((M // seq, DIFF_HEADS, LANES, seq), BF16),
            jax.ShapeDtypeStruct((M // seq, DIFF_HEADS, LANES, seq), BF16),
        ],
        scratch_shapes=[pltpu.VMEM((2, tm, D), BF16)],
        compiler_params=pltpu.CompilerParams(
            dimension_semantics=("arbitrary",), vmem_limit_bytes=VMEM_LIMIT),
        name="in_proj",
    )(x2, x2, mod3, mod3, mod3, mod3, g_pre.reshape(1, D), w_in_bf, cos, sin)


def _attn_kernel(lq1_ref, lk1_ref, lq2_ref, lk2_ref, subg_ref, q_ref, k_ref, v_ref, g_ref,
                 o_ref, vt_ref, qcat_ref, acc_ref, *, seq, tq, lam_init):
    tk, hq, nq = tq, tq // 2, seq // tq
    lam = (jnp.exp(jnp.sum(lq1_ref[...] * lk1_ref[...], axis=-1, keepdims=True))
           - jnp.exp(jnp.sum(lq2_ref[...] * lk2_ref[...], axis=-1, keepdims=True))
           + lam_init)

    dim = lax.broadcasted_iota(jnp.int32, (LANES, tq), 0)
    first_map = (dim // (DIFF_QK_DIM // 2)) % 2 == 0

    def stage_block(c):
        cols = slice(c * tq, (c + 1) * tq)
        vt_ref[c, :DIFF_V_DIM, :] = v_ref[:, cols]
        vt_ref[c, DIFF_V_DIM:, :] = jnp.ones((ONES_ROWS, tk), BF16)
        q = q_ref[:, cols]
        zero = jnp.zeros_like(q)
        q0, q1 = jnp.where(first_map, q, zero), jnp.where(first_map, zero, q)
        qcat_ref[c] = jnp.concatenate([q0[:, :hq], q1[:, :hq], q0[:, hq:], q1[:, hq:]], axis=1)

    n_groups = 2 * tq // hq
    key_c = lax.broadcasted_iota(jnp.int32, (hq, hq), 0) // CHUNK
    row_c = lax.broadcasted_iota(jnp.int32, (hq, hq), 1) // CHUNK
    causal = key_c <= row_c

    def scores(tile):
        kind, i, j, c = tile
        qc = qcat_ref[i, :, c * hq:(c + 1) * hq]
        if kind == "full":
            keys = k_ref[j * tk:(j + 1) * tk, :]
        elif kind == "diag_a":
            keys = k_ref[i * tk:i * tk + hq, :]
        else:
            keys = k_ref[i * tk + hq:(i + 1) * tk, :]
        return jnp.dot(keys, qc, preferred_element_type=F32)

    def softmax_pv(tile, s, m):
        kind, i, j, c = tile
        acc = acc_ref.at[i % 2]
        cols = slice(c * hq, (c + 1) * hq)
        if kind == "full":
            vt = vt_ref[j]
        elif kind == "diag_a":
            vt = vt_ref[i, :, :hq]
            if c < n_groups // 2:
                s = jnp.where(causal, s, NEG_INF)
        else:
            vt = vt_ref[i, :, hq:]
            s = jnp.where(causal, s, NEG_INF)
        m_new = jnp.max(s, axis=0, keepdims=True)
        if m is not None:
            m_new = jnp.maximum(m, m_new)
        p = jnp.exp2(s - m_new).astype(BF16)
        pv = jnp.dot(vt, p, preferred_element_type=F32)
        if m is None:
            acc[:, cols] = pv
        else:
            acc[:, cols] = acc[:, cols] * jnp.exp2(m - m_new) + pv
        return m_new

    def finish(i, half):
        acc = acc_ref.at[i % 2]
        o = []
        for c in (2 * half, 2 * half + 1):
            cols = slice(c * hq, (c + 1) * hq)
            o.append(acc[:DIFF_V_DIM, cols] / acc[DIFF_V_DIM:DIFF_V_DIM + 1, cols])
        att = (o[0] - lam * o[1]).T
        ms = jnp.mean(att * att, axis=-1, keepdims=True)
        y = att * lax.rsqrt(ms + EPS) * subg_ref[...] * (1.0 - lam_init)
        rows = slice(i * tq + half * hq, i * tq + (half + 1) * hq)
        o_ref[rows, :] = (y * _silu(g_ref[rows, :].astype(F32))).astype(BF16)

    tiles = []
    for i in range(nq):
        for j in range(i):
            tiles += [("full", i, j, c) for c in range(n_groups)]
        tiles += [("diag_a", i, i, c) for c in range(n_groups)]
        tiles += [("diag_b", i, i, c) for c in range(n_groups // 2, n_groups)]

    staged = set()

    def issue_scores(tile):
        if tile[1] not in staged:
            staged.add(tile[1])
            stage_block(tile[1])
        return scores(tile)

    pending = [issue_scores(t) for t in tiles[:SCORE_LOOKAHEAD]]
    m = {}
    for n, tile in enumerate(tiles):
        if n + SCORE_LOOKAHEAD < len(tiles):
            pending.append(issue_scores(tiles[n + SCORE_LOOKAHEAD]))
        kind, i, _, c = tile
        m[c] = softmax_pv(tile, pending.pop(0), m.get(c))
        if kind == "diag_a" and c == n_groups // 2 - 1:
            finish(i, 0)
        if kind == "diag_b" and c == n_groups - 1:
            finish(i, 1)
            m = {}


def _diff_attention(z3, q_t, v_t, lq1, lk1, lq2, lk2, subln_g, *, lam_init):
    B, S, _ = z3.shape
    tq = ATTN_TQ
    k_col = (2 * D_POOL) // LANES
    g_col = k_col + D_QK // LANES
    vec = lambda n: pl.BlockSpec((1, n), lambda b, h: (0, 0))
    head = lambda col: pl.BlockSpec((None, S, LANES), lambda b, h: (b, 0, col + h))
    head_t = pl.BlockSpec((None, None, LANES, S), lambda b, h: (b, h, 0, 0))
    kern = functools.partial(_attn_kernel, seq=S, tq=tq, lam_init=lam_init)
    return pl.pallas_call(
        kern,
        grid=(B, DIFF_HEADS),
        in_specs=[vec(DIFF_QK_DIM)] * 4 + [vec(DIFF_V_DIM),
                  head_t, head(k_col), head_t, head(g_col)],
        out_specs=pl.BlockSpec((None, S, LANES), lambda b, h: (b, 0, h)),
        out_shape=jax.ShapeDtypeStruct((B, S, D_DIFF), BF16),
        scratch_shapes=[pltpu.VMEM((S // tq, DIFF_V_DIM + ONES_ROWS, tq), BF16),
                        pltpu.VMEM((S // tq, LANES, 2 * tq), BF16),
                        pltpu.VMEM((2, DIFF_V_DIM + ONES_ROWS, 2 * tq), F32)],
        compiler_params=pltpu.CompilerParams(
            dimension_semantics=("parallel", "parallel"), vmem_limit_bytes=VMEM_LIMIT),
        name="diff_attention",
    )(lq1.reshape(1, -1), lk1.reshape(1, -1), lq2.reshape(1, -1), lk2.reshape(1, -1),
      subln_g.reshape(1, -1), q_t, z3, v_t, z3)


def _pool_bands(tm):
    lag = np.arange(tm)[:, None] - np.arange(tm)[None, :]
    hlag = np.arange(POOL_HALO)[:, None] + POOL_HALO - np.arange(POOL_HALO)[None, :]
    band = np.stack([(lag >= 0) & (lag < w) for w in POOL_WINDOWS]).astype(np.float32)
    hband = np.stack([hlag < w for w in POOL_WINDOWS]).astype(np.float32)
    return jnp.asarray(band, BF16), jnp.asarray(hband, BF16)


def _out_proj_kernel(u0_ref, gp0_ref, un_ref, halo_ref, gpn_ref, d_ref, band_ref, hband_ref, wp_ref,
                     ps_ref, w_ref, x_ref, gate_ref, gpost_ref, o_ref, mix_ref, y_ref, *, per_seq):
    i = pl.program_id(0)
    tm = band_ref.shape[1]
    gd = POOL_GROUP_DIM
    n_chunks = len(POOL_WINDOWS)
    cw = w_ref.shape[1] // n_chunks
    part = tm // n_chunks

    def pool_sums(u_ref_, g, halo):
        cols = slice(g * gd, (g + 1) * gd)
        win = jnp.dot(band_ref[g], u_ref_[:, cols], preferred_element_type=F32)
        if halo is None:
            return win
        halo_rows, seq_start = halo
        hwin = jnp.dot(hband_ref[g], halo_rows[:, cols], preferred_element_type=F32)
        hwin = jnp.where(seq_start, 0.0, hwin)
        return jnp.concatenate([win[:POOL_HALO] + hwin, win[POOL_HALO:]], axis=0)

    def pool_mix(u_ref_, gp_ref_, g, win, pos0, slot):
        cols = slice(g * gd, (g + 1) * gd)
        t1 = pos0 + lax.broadcasted_iota(jnp.int32, (tm, 1), 0) + 1
        count = jnp.minimum(t1, POOL_WINDOWS[g]).astype(F32)
        pooled = win / count - u_ref_[:, cols].astype(F32)
        mixed = (jnp.dot(pooled.astype(BF16), wp_ref[g], preferred_element_type=F32)
                 * ps_ref[:, cols])
        mix_ref[slot, :, cols] = (mixed * _silu(gp_ref_[:, cols].astype(F32))).astype(BF16)

    def epilogue_part(slot, n):
        rows = slice(n * part, (n + 1) * part)
        yq = [y_ref[slot, rows, c * cw:(c + 1) * cw] for c in range(n_chunks)]
        ms = sum(jnp.sum(v * v, axis=-1, keepdims=True) for v in yq) / w_ref.shape[1]
        scale = lax.rsqrt(ms + EPS)
        for c, v in enumerate(yq):
            cols = slice(c * cw, (c + 1) * cw)
            yn = v * scale * gpost_ref[:, cols]
            o_ref[rows, cols] = x_ref[rows, cols] + (1.0 + gate_ref[0, :, cols]) * yn

    @pl.when(i == 0)
    def _():
        for g in range(n_chunks):
            pool_mix(u0_ref, gp0_ref, g, pool_sums(u0_ref, g, None), 0, 0)
        y_ref[1] = jnp.zeros(y_ref.shape[1:], F32)

    def step(cur):
        nxt = 1 - cur
        it_next = (i + 1) % per_seq
        halo = (halo_ref, it_next == 0)
        mix_ref[cur, :, D_POOL:] = d_ref[...]
        def chunk(n):
            cols = slice(n * cw, (n + 1) * cw)
            y_ref[cur, :, cols] = jnp.dot(mix_ref[cur], w_ref[:, cols], preferred_element_type=F32)

        def mix(g, win):
            pool_mix(un_ref, gpn_ref, g, win, it_next * tm, nxt)

        win = {g: pool_sums(un_ref, g, halo) for g in (0, 1)}
        chunk(0)
        epilogue_part(nxt, 0)
        mix(0, win[0])
        epilogue_part(nxt, 1)
        win[2] = pool_sums(un_ref, 2, halo)
        chunk(1)
        mix(1, win[1])
        epilogue_part(nxt, 2)
        win[3] = pool_sums(un_ref, 3, halo)
        chunk(2)
        mix(2, win[2])
        epilogue_part(nxt, 3)
        mix(3, win[3])
        chunk(3)

    for parity in range(2):
        pl.when(i % 2 == parity)(functools.partial(step, parity))


def _out_proj(z2, diff_out, w_pool_bf, pool_scale, w_out_bf, x2, mod3, g_post, *, seq, layer):
    M, D = x2.shape
    tm = 256
    per_seq = seq // tm
    n_tiles = M // tm
    band, hband = _pool_bands(tm)
    const = lambda shape: pl.BlockSpec(shape, lambda i: (0,) * len(shape))
    of_layer = lambda shape, **kw: pl.BlockSpec((None,) + shape,
                                                lambda i: (layer,) + (0,) * len(shape), **kw)
    cur = lambda i: jnp.minimum(i, n_tiles - 1)
    nxt = lambda i: jnp.minimum(i + 1, n_tiles - 1)
    prv = lambda i: jnp.maximum(i - 1, 0)
    kern = functools.partial(_out_proj_kernel, per_seq=per_seq)
    return pl.pallas_call(
        kern,
        grid=(n_tiles + 1,),
        in_specs=[
            pl.BlockSpec((tm, D_POOL), lambda i: (0, 0)),
            pl.BlockSpec((tm, D_POOL), lambda i: (0, 1)),
            pl.BlockSpec((tm, D_POOL), lambda i: (nxt(i), 0)),
            pl.BlockSpec((POOL_HALO, D_POOL),
                         lambda i: (nxt(i) * (tm // POOL_HALO) - 1, 0)),
            pl.BlockSpec((tm, D_POOL), lambda i: (nxt(i), 1)),
            pl.BlockSpec((tm, D_DIFF), lambda i: (cur(i), 0)),
            const(band.shape), const(hband.shape), of_layer(w_pool_bf.shape[1:]),
            const((1, D_POOL)), of_layer((D, D), pipeline_mode=pl.Buffered(1)),
            pl.BlockSpec((tm, D), lambda i: (prv(i), 0)),
            pl.BlockSpec((1, 1, D), lambda i: (prv(i) // per_seq, 0, 2)),
            const((1, D)),
        ],
        out_specs=pl.BlockSpec((tm, D), lambda i: (prv(i), 0)),
        out_shape=jax.ShapeDtypeStruct((M, D), F32),
        scratch_shapes=[pltpu.VMEM((2, tm, D), BF16),
                        pltpu.VMEM((2, tm, D), F32)],
        compiler_params=pltpu.CompilerParams(
            dimension_semantics=("arbitrary",), vmem_limit_bytes=VMEM_LIMIT),
        name="out_proj",
    )(z2, z2, z2, z2, z2, diff_out, band, hband, w_pool_bf, pool_scale.reshape(1, D_POOL),
      w_out_bf, x2, mod3, g_post.reshape(1, D))


def _w_in_prep_kernel(w_ref, o_ref, *, tn):
    j = pl.program_id(1)
    first_qk = (2 * D_POOL) // tn
    is_qk = jnp.logical_and(j >= first_qk, j < first_qk + (2 * D_QK) // tn)

    @pl.when(is_qk)
    def _():
        quarter = DIFF_QK_DIM // 2
        blk = lax.broadcasted_iota(jnp.int32, (w_ref.shape[0], LANES), 1) // quarter
        for g in range(tn // LANES):
            x = w_ref[:, g * LANES:(g + 1) * LANES]
            y = jnp.where(blk == 1, pltpu.roll(x, LANES - quarter, 1),
                          jnp.where(blk == 2, pltpu.roll(x, quarter, 1), x))
            o_ref[:, g * LANES:(g + 1) * LANES] = y.astype(BF16)

    @pl.when(jnp.logical_not(is_qk))
    def _():
        o_ref[...] = w_ref[...].astype(BF16)


def _w_in_prep(w_in):
    depth, d, n = w_in.shape
    tn = 512
    return pl.pallas_call(
        functools.partial(_w_in_prep_kernel, tn=tn),
        grid=(depth, n // tn),
        in_specs=[pl.BlockSpec((None, d, tn), lambda l, j: (l, 0, j))],
        out_specs=pl.BlockSpec((None, d, tn), lambda l, j: (l, 0, j)),
        out_shape=jax.ShapeDtypeStruct((depth, d, n), BF16),
        compiler_params=pltpu.CompilerParams(
            dimension_semantics=("parallel", "parallel"), vmem_limit_bytes=VMEM_LIMIT),
        name="w_in_prep",
    )(w_in)


def kernel(x, c, positions, w_ada, b_ada, g_pre, w_in, w_pool, pool_scale, lambda_q1, lambda_k1,
           lambda_q2, lambda_k2, subln_g, w_out, g_post):
    B, S, D = x.shape
    depth = w_in.shape[0]
    cos, sin = _rope_tables(positions)
    mod = _ada_modulation(c, w_ada, b_ada)
    w_in_bf = _w_in_prep(w_in)
    w_pool_bf = w_pool.astype(BF16)
    w_out_bf = w_out.astype(BF16)
    x2 = x.reshape(B * S, D)
    for l in range(depth):
        lam_init = 0.8 - 0.6 * math.exp(-0.3 * l)
        mod3 = mod[l].reshape(B, 1, 3 * D)
        z2, q_t, v_t = _in_proj(x2, mod3, g_pre[l], w_in_bf, cos, sin, seq=S, layer=l)
        diff_out = _diff_attention(z2.reshape(B, S, D_Z), q_t, v_t, lambda_q1[l], lambda_k1[l],
                                   lambda_q2[l], lambda_k2[l], subln_g[l], lam_init=lam_init)
        x2 = _out_proj(z2, diff_out.reshape(B * S, D_DIFF), w_pool_bf, pool_scale[l],
                       w_out_bf, x2, mod3, g_post[l], seq=S, layer=l)
    return x2.reshape(B, S, D)
```

```python
import functools
import math

import jax
import jax.numpy as jnp
from jax import lax
from jax.experimental import pallas as pl
from jax.experimental.pallas import tpu as pltpu
import numpy as np

D_MODEL = 2048
CHUNK = 64
D_POOL = 1024
D_DIFF = 1024
POOL_WINDOWS = (2, 4, 8, 16)
N_POOL_GROUPS = 4
POOL_GROUP_DIM = 256
DIFF_HEADS = 8
DIFF_V_DIM = 128
DIFF_QK_DIM = 64
D_QK = 1024
D_IN = 6144
D_Z = 2 * D_POOL + D_QK + D_DIFF
ROPE_THETA = 10000.0
EPS = 1e-6
NEG_INF = -1e30

LANES = 128
ONES_ROWS = 16
ATTN_TQ = 512
SCORE_LOOKAHEAD = 3
POOL_HALO = 16
VMEM_LIMIT = 56 * 1024 * 1024
LOG2_E = math.log2(math.e)

F32 = jnp.float32
BF16 = jnp.bfloat16


def _silu(g):
    return g * jax.nn.sigmoid(g)


def _rope_table_kernel(pos_ref, freq_ref, sign_ref, cos_ref, sin_ref):
    ang = pos_ref[0].astype(F32) * freq_ref[...]
    cos_ref[0] = jnp.cos(ang)
    sin_ref[0] = jnp.sin(ang) * sign_ref[...]


def _rope_tables(positions):
    B, S = positions.shape
    ts = 512
    inv_freq = 1.0 / (ROPE_THETA ** (jnp.arange(0, DIFF_QK_DIM, 2, dtype=F32) / DIFF_QK_DIM))
    freq = jnp.tile(inv_freq, LANES // (DIFF_QK_DIM // 2))[None, :]
    sign = jnp.where(jnp.arange(LANES) < LANES // 2, -1.0, 1.0).astype(F32)[None, :]
    cos, sin = pl.pallas_call(
        _rope_table_kernel,
        grid=(B, S // ts),
        in_specs=[
            pl.BlockSpec((1, ts, 1), lambda b, i: (b, i, 0)),
            pl.BlockSpec((1, LANES), lambda b, i: (0, 0)),
            pl.BlockSpec((1, LANES), lambda b, i: (0, 0)),
        ],
        out_specs=[
            pl.BlockSpec((1, ts, LANES), lambda b, i: (b, i, 0)),
            pl.BlockSpec((1, ts, LANES), lambda b, i: (b, i, 0)),
        ],
        out_shape=[jax.ShapeDtypeStruct((B, S, LANES), F32)] * 2,
        name="rope_tables",
    )(positions.reshape(B, S, 1), freq, sign)
    return cos.reshape(B * S, LANES), sin.reshape(B * S, LANES)


def _ada_kernel(c_ref, w_ref, b_ref, o_ref):
    o_ref[0] = jnp.dot(c_ref[...], w_ref[0], preferred_element_type=F32) + b_ref[0]


def _ada_modulation(c, w_ada, b_ada):
    depth, d, n = w_ada.shape
    B = c.shape[0]
    rows = 8
    c_pad = jnp.zeros((rows, d), F32).at[:B].set(c)
    tn = 1024
    out = pl.pallas_call(
        _ada_kernel,
        grid=(depth, n // tn),
        in_specs=[
            pl.BlockSpec((rows, d), lambda l, j: (0, 0)),
            pl.BlockSpec((1, d, tn), lambda l, j: (l, 0, j)),
            pl.BlockSpec((1, 1, tn), lambda l, j: (l, 0, j)),
        ],
        out_specs=pl.BlockSpec((1, rows, tn), lambda l, j: (l, 0, j)),
        out_shape=jax.ShapeDtypeStruct((depth, rows, n), F32),
        compiler_params=pltpu.CompilerParams(vmem_limit_bytes=VMEM_LIMIT),
        name="ada_modulation",
    )(c_pad, w_ada, b_ada.reshape(depth, 1, n))
    return out[:, :B]


def _in_proj_kernel(x0_ref, xn_ref, shift0_ref, scale0_ref, shiftn_ref, scalen_ref, gpre_ref,
                    w_ref, cos_ref, sin_ref, z_ref, qt_ref, vt_ref, h_ref, *, tn):
    i = pl.program_id(0)
    n_col_tiles = w_ref.shape[1] // tn
    q_tile = (2 * D_POOL) // tn
    k_tile = (2 * D_POOL + D_QK) // tn
    v_tile = (2 * D_POOL + 2 * D_QK) // tn

    def normed(x_ref, shift_ref, scale_ref):
        x = x_ref[...]
        ms = jnp.mean(x * x, axis=-1, keepdims=True)
        y = x * lax.rsqrt(ms + EPS) * gpre_ref[...]
        return (y * (1.0 + scale_ref[0]) + shift_ref[0]).astype(BF16)

    @pl.when(i == 0)
    def _():
        h_ref[0] = normed(x0_ref, shift0_ref, scale0_ref)

    def step(cur):
        h_ref[1 - cur] = normed(xn_ref, shiftn_ref, scalen_ref)
        h = h_ref[cur]
        z_tile = 0
        for jt in range(n_col_tiles):
            acc = jnp.dot(h, w_ref[:, jt * tn:(jt + 1) * tn], preferred_element_type=F32)
            if jt in (q_tile, k_tile):
                qk_scale = LOG2_E / math.sqrt(DIFF_QK_DIM) if jt == q_tile else 1.0
                cos = cos_ref[...] * qk_scale
                sin = sin_ref[...] * qk_scale
                for g in range(tn // LANES):
                    xg = acc[:, g * LANES:(g + 1) * LANES]
                    rot = pltpu.roll(xg, LANES // 2, 1)
                    roped = xg * cos + rot * sin
                    if jt == q_tile:
                        qt_ref[g] = roped.T.astype(BF16)
                    else:
                        col = z_tile * tn + g * LANES
                        z_ref[:, col:col + LANES] = roped.astype(BF16)
            elif jt == v_tile:
                for g in range(tn // LANES):
                    vt_ref[g] = acc[:, g * LANES:(g + 1) * LANES].T.astype(BF16)
            else:
                z_ref[:, z_tile * tn:(z_tile + 1) * tn] = acc.astype(BF16)
            if jt not in (q_tile, v_tile):
                z_tile += 1

    for parity in range(2):
        pl.when(i % 2 == parity)(functools.partial(step, parity))


def _in_proj(x2, mod3, g_pre, w_in_bf, cos, sin, *, seq, layer):
    M, D = x2.shape
    N = w_in_bf.shape[2]
    tm, tn = 256, 1024
    per_seq = seq // tm
    last = M // tm - 1
    nxt = lambda i: jnp.minimum(i + 1, last)
    kern = functools.partial(_in_proj_kernel, tn=tn)
    return pl.pallas_call(
        kern,
        grid=(M // tm,),
        in_specs=[
            pl.BlockSpec((tm, D), lambda i: (0, 0)),
            pl.BlockSpec((tm, D), lambda i: (nxt(i), 0)),
            pl.BlockSpec((1, 1, D), lambda i: (0, 0, 0)),
            pl.BlockSpec((1, 1, D), lambda i: (0, 0, 1)),
            pl.BlockSpec((1, 1, D), lambda i: (nxt(i) // per_seq, 0, 0)),
            pl.BlockSpec((1, 1, D), lambda i: (nxt(i) // per_seq, 0, 1)),
            pl.BlockSpec((1, D), lambda i: (0, 0)),
            pl.BlockSpec((None, D, N), lambda i: (layer, 0, 0), pipeline_mode=pl.Buffered(1)),
            pl.BlockSpec((tm, LANES), lambda i: (i, 0)),
            pl.BlockSpec((tm, LANES), lambda i: (i, 0)),
        ],
        out_specs=[
            pl.BlockSpec((tm, D_Z), lambda i: (i, 0)),
            pl.BlockSpec((None, DIFF_HEADS, LANES, tm), lambda i: (i // per_seq, 0, 0, i % per_seq)),
            pl.BlockSpec((None, DIFF_HEADS, LANES, tm), lambda i: (i // per_seq, 0, 0, i % per_seq)),
        ],
        out_shape=[
            jax.ShapeDtypeStruct((M, D_Z), BF16),
            jax.ShapeDtypeStruct((M // seq, DIFF_HEADS, LANES, seq), BF16),
            jax.ShapeDtypeStruct((M // seq, DIFF_HEADS, LANES, seq), BF16),
        ],
        scratch_shapes=[pltpu.VMEM((2, tm, D), BF16)],
        compiler_params=pltpu.CompilerParams(
            dimension_semantics=("arbitrary",), vmem_limit_bytes=VMEM_LIMIT),
        name="in_proj",
    )(x2, x2, mod3, mod3, mod3, mod3, g_pre.reshape(1, D), w_in_bf, cos, sin)


def _attn_kernel(lq1_ref, lk1_ref, lq2_ref, lk2_ref, subg_ref, q_ref, k_ref, v_ref, g_ref,
                 o_ref, vt_ref, qcat_ref, acc_ref, *, seq, tq, lam_init):
    tk, hq, nq = tq, tq // 2, seq // tq
    lam = (jnp.exp(jnp.sum(lq1_ref[...] * lk1_ref[...], axis=-1, keepdims=True))
           - jnp.exp(jnp.sum(lq2_ref[...] * lk2_ref[...], axis=-1, keepdims=True))
           + lam_init)

    dim = lax.broadcasted_iota(jnp.int32, (LANES, tq), 0)
    first_map = (dim // (DIFF_QK_DIM // 2)) % 2 == 0
    for c in range(nq):
        cols = slice(c * tq, (c + 1) * tq)
        vt_ref[c, :DIFF_V_DIM, :] = v_ref[:, cols]
        vt_ref[c, DIFF_V_DIM:, :] = jnp.ones((ONES_ROWS, tk), BF16)
        q = q_ref[:, cols]
        zero = jnp.zeros_like(q)
        q0, q1 = jnp.where(first_map, q, zero), jnp.where(first_map, zero, q)
        qcat_ref[c] = jnp.concatenate([q0[:, :hq], q1[:, :hq], q0[:, hq:], q1[:, hq:]], axis=1)

    n_groups = 2 * tq // hq
    key_c = lax.broadcasted_iota(jnp.int32, (hq, hq), 0) // CHUNK
    row_c = lax.broadcasted_iota(jnp.int32, (hq, hq), 1) // CHUNK
    causal = key_c <= row_c

    def scores(tile):
        kind, i, j, c = tile
        qc = qcat_ref[i, :, c * hq:(c + 1) * hq]
        if kind == "full":
            keys = k_ref[j * tk:(j + 1) * tk, :]
        elif kind == "diag_a":
            keys = k_ref[i * tk:i * tk + hq, :]
        else:
            keys = k_ref[i * tk + hq:(i + 1) * tk, :]
        return jnp.dot(keys, qc, preferred_element_type=F32)

    def softmax_pv(tile, s, m):
        kind, i, j, c = tile
        acc = acc_ref.at[i % 2]
        cols = slice(c * hq, (c + 1) * hq)
        if kind == "full":
            vt = vt_ref[j]
        elif kind == "diag_a":
            vt = vt_ref[i, :, :hq]
            if c < n_groups // 2:
                s = jnp.where(causal, s, NEG_INF)
        else:
            vt = vt_ref[i, :, hq:]
            s = jnp.where(causal, s, NEG_INF)
        m_new = jnp.max(s, axis=0, keepdims=True)
        if m is not None:
            m_new = jnp.maximum(m, m_new)
        p = jnp.exp2(s - m_new).astype(BF16)
        pv = jnp.dot(vt, p, preferred_element_type=F32)
        if m is None:
            acc[:, cols] = pv
        else:
            acc[:, cols] = acc[:, cols] * jnp.exp2(m - m_new) + pv
        return m_new

    def finish(i):
        acc = acc_ref.at[i % 2]
        o = acc[:DIFF_V_DIM, :] / acc[DIFF_V_DIM:DIFF_V_DIM + 1, :]
        o0 = jnp.concatenate([o[:, :hq], o[:, tq:tq + hq]], axis=1)
        o1 = jnp.concatenate([o[:, hq:tq], o[:, tq + hq:]], axis=1)
        att = (o0 - lam * o1).T
        ms = jnp.mean(att * att, axis=-1, keepdims=True)
        y = att * lax.rsqrt(ms + EPS) * subg_ref[...] * (1.0 - lam_init)
        rows = slice(i * tq, (i + 1) * tq)
        o_ref[rows, :] = (y * _silu(g_ref[rows, :].astype(F32))).astype(BF16)

    tiles = []
    for i in range(nq):
        for j in range(i):
            tiles += [("full", i, j, c) for c in range(n_groups)]
        tiles += [("diag_a", i, i, c) for c in range(n_groups)]
        tiles += [("diag_b", i, i, c) for c in range(n_groups // 2, n_groups)]
    pending = [scores(t) for t in tiles[:SCORE_LOOKAHEAD]]
    m = {}
    for n, tile in enumerate(tiles):
        if n + SCORE_LOOKAHEAD < len(tiles):
            pending.append(scores(tiles[n + SCORE_LOOKAHEAD]))
        kind, i, _, c = tile
        m[c] = softmax_pv(tile, pending.pop(0), m.get(c))
        if kind == "diag_b" and c == n_groups - 1:
            finish(i)
            m = {}


def _diff_attention(z3, q_t, v_t, lq1, lk1, lq2, lk2, subln_g, *, lam_init):
    B, S, _ = z3.shape
    tq = ATTN_TQ
    k_col = (2 * D_POOL) // LANES
    g_col = k_col + D_QK // LANES
    vec = lambda n: pl.BlockSpec((1, n), lambda b, h: (0, 0))
    head = lambda col: pl.BlockSpec((None, S, LANES), lambda b, h: (b, 0, col + h))
    head_t = pl.BlockSpec((None, None, LANES, S), lambda b, h: (b, h, 0, 0))
    kern = functools.partial(_attn_kernel, seq=S, tq=tq, lam_init=lam_init)
    return pl.pallas_call(
        kern,
        grid=(B, DIFF_HEADS),
        in_specs=[vec(DIFF_QK_DIM)] * 4 + [vec(DIFF_V_DIM),
                  head_t, head(k_col), head_t, head(g_col)],
        out_specs=pl.BlockSpec((None, S, LANES), lambda b, h: (b, 0, h)),
        out_shape=jax.ShapeDtypeStruct((B, S, D_DIFF), BF16),
        scratch_shapes=[pltpu.VMEM((S // tq, DIFF_V_DIM + ONES_ROWS, tq), BF16),
                        pltpu.VMEM((S // tq, LANES, 2 * tq), BF16),
                        pltpu.VMEM((2, DIFF_V_DIM + ONES_ROWS, 2 * tq), F32)],
        compiler_params=pltpu.CompilerParams(
            dimension_semantics=("parallel", "parallel"), vmem_limit_bytes=VMEM_LIMIT),
        name="diff_attention",
    )(lq1.reshape(1, -1), lk1.reshape(1, -1), lq2.reshape(1, -1), lk2.reshape(1, -1),
      subln_g.reshape(1, -1), q_t, z3, v_t, z3)


def _pool_bands(tm):
    lag = np.arange(tm)[:, None] - np.arange(tm)[None, :]
    hlag = np.arange(POOL_HALO)[:, None] + POOL_HALO - np.arange(POOL_HALO)[None, :]
    band = np.stack([(lag >= 0) & (lag < w) for w in POOL_WINDOWS]).astype(np.float32)
    hband = np.stack([hlag < w for w in POOL_WINDOWS]).astype(np.float32)
    return jnp.asarray(band, BF16), jnp.asarray(hband, BF16)


def _out_proj_kernel(u0_ref, gp0_ref, un_ref, halo_ref, gpn_ref, d_ref, band_ref, hband_ref, wp_ref,
                     ps_ref, w_ref, x_ref, gate_ref, gpost_ref, o_ref, mix_ref, y_ref, *, per_seq):
    i = pl.program_id(0)
    tm = band_ref.shape[1]
    gd = POOL_GROUP_DIM
    n_chunks = len(POOL_WINDOWS)
    cw = w_ref.shape[1] // n_chunks
    part = tm // n_chunks

    def pool_sums(u_ref_, g, halo):
        cols = slice(g * gd, (g + 1) * gd)
        win = jnp.dot(band_ref[g], u_ref_[:, cols], preferred_element_type=F32)
        if halo is None:
            return win
        halo_rows, seq_start = halo
        hwin = jnp.dot(hband_ref[g], halo_rows[:, cols], preferred_element_type=F32)
        hwin = jnp.where(seq_start, 0.0, hwin)
        return jnp.concatenate([win[:POOL_HALO] + hwin, win[POOL_HALO:]], axis=0)

    def pool_mix(u_ref_, gp_ref_, g, win, pos0, slot):
        cols = slice(g * gd, (g + 1) * gd)
        t1 = pos0 + lax.broadcasted_iota(jnp.int32, (tm, 1), 0) + 1
        count = jnp.minimum(t1, POOL_WINDOWS[g]).astype(F32)
        pooled = win / count - u_ref_[:, cols].astype(F32)
        mixed = (jnp.dot(pooled.astype(BF16), wp_ref[g], preferred_element_type=F32)
                 * ps_ref[:, cols])
        mix_ref[slot, :, cols] = (mixed * _silu(gp_ref_[:, cols].astype(F32))).astype(BF16)

    def epilogue_part(slot, n):
        rows = slice(n * part, (n + 1) * part)
        yq = [y_ref[slot, rows, c * cw:(c + 1) * cw] for c in range(n_chunks)]
        ms = sum(jnp.sum(v * v, axis=-1, keepdims=True) for v in yq) / w_ref.shape[1]
        scale = lax.rsqrt(ms + EPS)
        for c, v in enumerate(yq):
            cols = slice(c * cw, (c + 1) * cw)
            yn = v * scale * gpost_ref[:, cols]
            o_ref[rows, cols] = x_ref[rows, cols] + (1.0 + gate_ref[0, :, cols]) * yn

    @pl.when(i == 0)
    def _():
        for g in range(n_chunks):
            pool_mix(u0_ref, gp0_ref, g, pool_sums(u0_ref, g, None), 0, 0)
        y_ref[1] = jnp.zeros(y_ref.shape[1:], F32)

    def step(cur):
        nxt = 1 - cur
        it_next = (i + 1) % per_seq
        halo = (halo_ref, it_next == 0)
        mix_ref[cur, :, D_POOL:] = d_ref[...]
        def chunk(n):
            cols = slice(n * cw, (n + 1) * cw)
            y_ref[cur, :, cols] = jnp.dot(mix_ref[cur], w_ref[:, cols], preferred_element_type=F32)

        def mix(g, win):
            pool_mix(un_ref, gpn_ref, g, win, it_next * tm, nxt)

        win = {g: pool_sums(un_ref, g, halo) for g in (0, 1)}
        chunk(0)
        epilogue_part(nxt, 0)
        mix(0, win[0])
        epilogue_part(nxt, 1)
        win[2] = pool_sums(un_ref, 2, halo)
        chunk(1)
        mix(1, win[1])
        epilogue_part(nxt, 2)
        win[3] = pool_sums(un_ref, 3, halo)
        chunk(2)
        mix(2, win[2])
        epilogue_part(nxt, 3)
        mix(3, win[3])
        chunk(3)

    for parity in range(2):
        pl.when(i % 2 == parity)(functools.partial(step, parity))


def _out_proj(z2, diff_out, w_pool_bf, pool_scale, w_out_bf, x2, mod3, g_post, *, seq, layer):
    M, D = x2.shape
    tm = 256
    per_seq = seq // tm
    n_tiles = M // tm
    band, hband = _pool_bands(tm)
    const = lambda shape: pl.BlockSpec(shape, lambda i: (0,) * len(shape))
    of_layer = lambda shape, **kw: pl.BlockSpec((None,) + shape,
                                                lambda i: (layer,) + (0,) * len(shape), **kw)
    cur = lambda i: jnp.minimum(i, n_tiles - 1)
    nxt = lambda i: jnp.minimum(i + 1, n_tiles - 1)
    prv = lambda i: jnp.maximum(i - 1, 0)
    kern = functools.partial(_out_proj_kernel, per_seq=per_seq)
    return pl.pallas_call(
        kern,
        grid=(n_tiles + 1,),
        in_specs=[
            pl.BlockSpec((tm, D_POOL), lambda i: (0, 0)),
            pl.BlockSpec((tm, D_POOL), lambda i: (0, 1)),
            pl.BlockSpec((tm, D_POOL), lambda i: (nxt(i), 0)),
            pl.BlockSpec((POOL_HALO, D_POOL),
                         lambda i: (nxt(i) * (tm // POOL_HALO) - 1, 0)),
            pl.BlockSpec((tm, D_POOL), lambda i: (nxt(i), 1)),
            pl.BlockSpec((tm, D_DIFF), lambda i: (cur(i), 0)),
            const(band.shape), const(hband.shape), of_layer(w_pool_bf.shape[1:]),
            const((1, D_POOL)), of_layer((D, D), pipeline_mode=pl.Buffered(1)),
            pl.BlockSpec((tm, D), lambda i: (prv(i), 0)),
            pl.BlockSpec((1, 1, D), lambda i: (prv(i) // per_seq, 0, 2)),
            const((1, D)),
        ],
        out_specs=pl.BlockSpec((tm, D), lambda i: (prv(i), 0)),
        out_shape=jax.ShapeDtypeStruct((M, D), F32),
        scratch_shapes=[pltpu.VMEM((2, tm, D), BF16),
                        pltpu.VMEM((2, tm, D), F32)],
        compiler_params=pltpu.CompilerParams(
            dimension_semantics=("arbitrary",), vmem_limit_bytes=VMEM_LIMIT),
        name="out_proj",
    )(z2, z2, z2, z2, z2, diff_out, band, hband, w_pool_bf, pool_scale.reshape(1, D_POOL),
      w_out_bf, x2, mod3, g_post.reshape(1, D))


def _w_in_prep_kernel(w_ref, o_ref, *, tn):
    j = pl.program_id(1)
    first_qk = (2 * D_POOL) // tn
    is_qk = jnp.logical_and(j >= first_qk, j < first_qk + (2 * D_QK) // tn)

    @pl.when(is_qk)
    def _():
        quarter = DIFF_QK_DIM // 2
        blk = lax.broadcasted_iota(jnp.int32, (w_ref.shape[0], LANES), 1) // quarter
        for g in range(tn // LANES):
            x = w_ref[:, g * LANES:(g + 1) * LANES]
            y = jnp.where(blk == 1, pltpu.roll(x, LANES - quarter, 1),
                          jnp.where(blk == 2, pltpu.roll(x, quarter, 1), x))
            o_ref[:, g * LANES:(g + 1) * LANES] = y.astype(BF16)

    @pl.when(jnp.logical_not(is_qk))
    def _():
        o_ref[...] = w_ref[...].astype(BF16)


def _w_in_prep(w_in):
    depth, d, n = w_in.shape
    tn = 512
    return pl.pallas_call(
        functools.partial(_w_in_prep_kernel, tn=tn),
        grid=(depth, n // tn),
        in_specs=[pl.BlockSpec((None, d, tn), lambda l, j: (l, 0, j))],
        out_specs=pl.BlockSpec((None, d, tn), lambda l, j: (l, 0, j)),
        out_shape=jax.ShapeDtypeStruct((depth, d, n), BF16),
        compiler_params=pltpu.CompilerParams(
            dimension_semantics=("parallel", "parallel"), vmem_limit_bytes=VMEM_LIMIT),
        name="w_in_prep",
    )(w_in)


def kernel(x, c, positions, w_ada, b_ada, g_pre, w_in, w_pool, pool_scale, lambda_q1, lambda_k1,
           lambda_q2, lambda_k2, subln_g, w_out, g_post):
    B, S, D = x.shape
    depth = w_in.shape[0]
    cos, sin = _rope_tables(positions)
    mod = _ada_modulation(c, w_ada, b_ada)
    w_in_bf = _w_in_prep(w_in)
    w_pool_bf = w_pool.astype(BF16)
    w_out_bf = w_out.astype(BF16)
    x2 = x.reshape(B * S, D)
    for l in range(depth):
        lam_init = 0.8 - 0.6 * math.exp(-0.3 * l)
        mod3 = mod[l].reshape(B, 1, 3 * D)
        z2, q_t, v_t = _in_proj(x2, mod3, g_pre[l], w_in_bf, cos, sin, seq=S, layer=l)
        diff_out = _diff_attention(z2.reshape(B, S, D_Z), q_t, v_t, lambda_q1[l], lambda_k1[l],
                                   lambda_q2[l], lambda_k2[l], subln_g[l], lam_init=lam_init)
        x2 = _out_proj(z2, diff_out.reshape(B * S, D_DIFF), w_pool_bf, pool_scale[l],
                       w_out_bf, x2, mod3, g_post[l], seq=S, layer=l)
    return x2.reshape(B, S, D)
```

```python
import functools
import math

import jax
import jax.numpy as jnp
from jax import lax
from jax.experimental import pallas as pl
from jax.experimental.pallas import tpu as pltpu
import numpy as np

D_MODEL = 2048
CHUNK = 64
D_POOL = 1024
D_DIFF = 1024
POOL_WINDOWS = (2, 4, 8, 16)
N_POOL_GROUPS = 4
POOL_GROUP_DIM = 256
DIFF_HEADS = 8
DIFF_V_DIM = 128
DIFF_QK_DIM = 64
D_QK = 1024
D_IN = 6144
D_Z = 2 * D_POOL + D_QK + D_DIFF
ROPE_THETA = 10000.0
EPS = 1e-6
NEG_INF = -1e30

LANES = 128
ONES_ROWS = 16
ATTN_TQ = 512
ATTN_FULL_KEYS = 256
SCORE_LOOKAHEAD = 7
POOL_HALO = 16
VMEM_LIMIT = 56 * 1024 * 1024
LOG2_E = math.log2(math.e)

F32 = jnp.float32
BF16 = jnp.bfloat16


def _silu(g):
    return g * jax.nn.sigmoid(g)


def _rope_table_kernel(pos_ref, freq_ref, sign_ref, cos_ref, sin_ref):
    ang = pos_ref[0].astype(F32) * freq_ref[...]
    cos_ref[0] = jnp.cos(ang)
    sin_ref[0] = jnp.sin(ang) * sign_ref[...]


def _rope_tables(positions):
    B, S = positions.shape
    ts = 512
    inv_freq = 1.0 / (ROPE_THETA ** (jnp.arange(0, DIFF_QK_DIM, 2, dtype=F32) / DIFF_QK_DIM))
    freq = jnp.tile(inv_freq, LANES // (DIFF_QK_DIM // 2))[None, :]
    sign = jnp.where(jnp.arange(LANES) < LANES // 2, -1.0, 1.0).astype(F32)[None, :]
    cos, sin = pl.pallas_call(
        _rope_table_kernel,
        grid=(B, S // ts),
        in_specs=[
            pl.BlockSpec((1, ts, 1), lambda b, i: (b, i, 0)),
            pl.BlockSpec((1, LANES), lambda b, i: (0, 0)),
            pl.BlockSpec((1, LANES), lambda b, i: (0, 0)),
        ],
        out_specs=[
            pl.BlockSpec((1, ts, LANES), lambda b, i: (b, i, 0)),
            pl.BlockSpec((1, ts, LANES), lambda b, i: (b, i, 0)),
        ],
        out_shape=[jax.ShapeDtypeStruct((B, S, LANES), F32)] * 2,
        name="rope_tables",
    )(positions.reshape(B, S, 1), freq, sign)
    return cos.reshape(B * S, LANES), sin.reshape(B * S, LANES)


def _ada_kernel(c_ref, w_ref, b_ref, o_ref):
    o_ref[0] = jnp.dot(c_ref[...], w_ref[0], preferred_element_type=F32) + b_ref[0]


def _ada_modulation(c, w_ada, b_ada):
    depth, d, n = w_ada.shape
    B = c.shape[0]
    rows = 8
    c_pad = jnp.zeros((rows, d), F32).at[:B].set(c)
    tn = 1024
    out = pl.pallas_call(
        _ada_kernel,
        grid=(depth, n // tn),
        in_specs=[
            pl.BlockSpec((rows, d), lambda l, j: (0, 0)),
            pl.BlockSpec((1, d, tn), lambda l, j: (l, 0, j)),
            pl.BlockSpec((1, 1, tn), lambda l, j: (l, 0, j)),
        ],
        out_specs=pl.BlockSpec((1, rows, tn), lambda l, j: (l, 0, j)),
        out_shape=jax.ShapeDtypeStruct((depth, rows, n), F32),
        compiler_params=pltpu.CompilerParams(vmem_limit_bytes=VMEM_LIMIT),
        name="ada_modulation",
    )(c_pad, w_ada, b_ada.reshape(depth, 1, n))
    return out[:, :B]


def _in_proj_kernel(x0_ref, xn_ref, shift0_ref, scale0_ref, shiftn_ref, scalen_ref, gpre_ref,
                    w_ref, cos_ref, sin_ref, z_ref, qt_ref, vt_ref, h_ref, *, tn):
    i = pl.program_id(0)
    n_col_tiles = w_ref.shape[1] // tn
    q_tile = (2 * D_POOL) // tn
    k_tile = (2 * D_POOL + D_QK) // tn
    v_tile = (2 * D_POOL + 2 * D_QK) // tn

    def normed(x_ref, shift_ref, scale_ref):
        x = x_ref[...]
        ms = jnp.mean(x * x, axis=-1, keepdims=True)
        y = x * lax.rsqrt(ms + EPS) * gpre_ref[...]
        return (y * (1.0 + scale_ref[0]) + shift_ref[0]).astype(BF16)

    @pl.when(i == 0)
    def _():
        h_ref[0] = normed(x0_ref, shift0_ref, scale0_ref)

    def step(cur):
        h_ref[1 - cur] = normed(xn_ref, shiftn_ref, scalen_ref)
        h = h_ref[cur]
        z_tile = 0
        for jt in range(n_col_tiles):
            acc = jnp.dot(h, w_ref[:, jt * tn:(jt + 1) * tn], preferred_element_type=F32)
            if jt in (q_tile, k_tile):
                qk_scale = LOG2_E / math.sqrt(DIFF_QK_DIM) if jt == q_tile else 1.0
                cos = cos_ref[...] * qk_scale
                sin = sin_ref[...] * qk_scale
                for g in range(tn // LANES):
                    xg = acc[:, g * LANES:(g + 1) * LANES]
                    rot = pltpu.roll(xg, LANES // 2, 1)
                    roped = xg * cos + rot * sin
                    if jt == q_tile:
                        qt_ref[g] = roped.T.astype(BF16)
                    else:
                        col = z_tile * tn + g * LANES
                        z_ref[:, col:col + LANES] = roped.astype(BF16)
            elif jt == v_tile:
                for g in range(tn // LANES):
                    vt_ref[g] = acc[:, g * LANES:(g + 1) * LANES].T.astype(BF16)
            else:
                z_ref[:, z_tile * tn:(z_tile + 1) * tn] = acc.astype(BF16)
            if jt not in (q_tile, v_tile):
                z_tile += 1

    for parity in range(2):
        pl.when(i % 2 == parity)(functools.partial(step, parity))


def _in_proj(x2, mod3, g_pre, w_in_bf, cos, sin, *, seq, layer):
    M, D = x2.shape
    N = w_in_bf.shape[2]
    tm, tn = 256, 1024
    per_seq = seq // tm
    last = M // tm - 1
    nxt = lambda i: jnp.minimum(i + 1, last)
    kern = functools.partial(_in_proj_kernel, tn=tn)
    return pl.pallas_call(
        kern,
        grid=(M // tm,),
        in_specs=[
            pl.BlockSpec((tm, D), lambda i: (0, 0)),
            pl.BlockSpec((tm, D), lambda i: (nxt(i), 0)),
            pl.BlockSpec((1, 1, D), lambda i: (0, 0, 0)),
            pl.BlockSpec((1, 1, D), lambda i: (0, 0, 1)),
            pl.BlockSpec((1, 1, D), lambda i: (nxt(i) // per_seq, 0, 0)),
            pl.BlockSpec((1, 1, D), lambda i: (nxt(i) // per_seq, 0, 1)),
            pl.BlockSpec((1, D), lambda i: (0, 0)),
            pl.BlockSpec((None, D, N), lambda i: (layer, 0, 0), pipeline_mode=pl.Buffered(1)),
            pl.BlockSpec((tm, LANES), lambda i: (i, 0)),
            pl.BlockSpec((tm, LANES), lambda i: (i, 0)),
        ],
        out_specs=[
            pl.BlockSpec((tm, D_Z), lambda i: (i, 0)),
            pl.BlockSpec((None, DIFF_HEADS, LANES, tm), lambda i: (i // per_seq, 0, 0, i % per_seq)),
            pl.BlockSpec((None, DIFF_HEADS, LANES, tm), lambda i: (i // per_seq, 0, 0, i % per_seq)),
        ],
        out_shape=[
            jax.ShapeDtypeStruct((M, D_Z), BF16),
            jax.ShapeDtypeStruct((M // seq, DIFF_HEADS, LANES, seq), BF16),
            jax.ShapeDtypeStruct((M // seq, DIFF_HEADS, LANES, seq), BF16),
        ],
        scratch_shapes=[pltpu.VMEM((2, tm, D), BF16)],
        compiler_params=pltpu.CompilerParams(
            dimension_semantics=("arbitrary",), vmem_limit_bytes=VMEM_LIMIT),
        name="in_proj",
    )(x2, x2, mod3, mod3, mod3, mod3, g_pre.reshape(1, D), w_in_bf, cos, sin)


def _attn_kernel(lq1_ref, lk1_ref, lq2_ref, lk2_ref, subg_ref, q_ref, k_ref, v_ref, g_ref,
                 o_ref, vt_ref, qcat_ref, acc_ref, *, seq, tq, lam_init):
    hq, nq = tq // 2, seq // tq
    lam = (jnp.exp(jnp.sum(lq1_ref[...] * lk1_ref[...], axis=-1, keepdims=True))
           - jnp.exp(jnp.sum(lq2_ref[...] * lk2_ref[...], axis=-1, keepdims=True))
           + lam_init)

    dim = lax.broadcasted_iota(jnp.int32, (LANES, tq), 0)
    first_map = (dim // (DIFF_QK_DIM // 2)) % 2 == 0
    for c in range(nq):
        cols = slice(c * tq, (c + 1) * tq)
        vt_ref[:DIFF_V_DIM, cols] = v_ref[:, cols]
        q = q_ref[:, cols]
        zero = jnp.zeros_like(q)
        q0, q1 = jnp.where(first_map, q, zero), jnp.where(first_map, zero, q)
        qcat_ref[c] = jnp.concatenate([q0[:, :hq], q1[:, :hq], q0[:, hq:], q1[:, hq:]], axis=1)
    vt_ref[DIFF_V_DIM:, :] = jnp.ones((ONES_ROWS, seq), BF16)

    n_groups = 2 * tq // hq
    key_c = lax.broadcasted_iota(jnp.int32, (hq, hq), 0) // CHUNK
    row_c = lax.broadcasted_iota(jnp.int32, (hq, hq), 1) // CHUNK
    causal = key_c <= row_c

    def scores(tile):
        _, i, k0, klen, c = tile
        return jnp.dot(k_ref[k0:k0 + klen, :], qcat_ref[i, :, c * hq:(c + 1) * hq],
                       preferred_element_type=F32)

    def softmax_pv(tile, s, m):
        kind, i, k0, klen, c = tile
        acc = acc_ref.at[i % 2]
        cols = slice(c * hq, (c + 1) * hq)
        if kind == "diag_b" or (kind == "diag_a" and c < n_groups // 2):
            s = jnp.where(causal, s, NEG_INF)
        m_new = jnp.max(s, axis=0, keepdims=True)
        if m is not None:
            m_new = jnp.maximum(m, m_new)
        p = jnp.exp2(s - m_new).astype(BF16)
        pv = jnp.dot(vt_ref[:, k0:k0 + klen], p, preferred_element_type=F32)
        if m is None:
            acc[:, cols] = pv
        else:
            acc[:, cols] = acc[:, cols] * jnp.exp2(m - m_new) + pv
        return m_new

    def finish(i):
        acc = acc_ref.at[i % 2]
        o = acc[:DIFF_V_DIM, :] / acc[DIFF_V_DIM:DIFF_V_DIM + 1, :]
        o0 = jnp.concatenate([o[:, :hq], o[:, tq:tq + hq]], axis=1)
        o1 = jnp.concatenate([o[:, hq:tq], o[:, tq + hq:]], axis=1)
        att = (o0 - lam * o1).T
        ms = jnp.mean(att * att, axis=-1, keepdims=True)
        y = att * lax.rsqrt(ms + EPS) * subg_ref[...] * (1.0 - lam_init)
        rows = slice(i * tq, (i + 1) * tq)
        o_ref[rows, :] = (y * _silu(g_ref[rows, :].astype(F32))).astype(BF16)

    tiles = []
    for i in range(nq):
        for k0 in range(0, i * tq, ATTN_FULL_KEYS):
            klen = min(ATTN_FULL_KEYS, i * tq - k0)
            tiles += [("full", i, k0, klen, c) for c in range(n_groups)]
        tiles += [("diag_a", i, i * tq, hq, c) for c in range(n_groups)]
        tiles += [("diag_b", i, i * tq + hq, hq, c) for c in range(n_groups // 2, n_groups)]
    pending = [scores(t) for t in tiles[:SCORE_LOOKAHEAD]]
    m = {}
    for n, tile in enumerate(tiles):
        if n + SCORE_LOOKAHEAD < len(tiles):
            pending.append(scores(tiles[n + SCORE_LOOKAHEAD]))
        kind, i, _, _, c = tile
        m[c] = softmax_pv(tile, pending.pop(0), m.get(c))
        if kind == "diag_b" and c == n_groups - 1:
            finish(i)
            m = {}


def _diff_attention(z3, q_t, v_t, lq1, lk1, lq2, lk2, subln_g, *, lam_init):
    B, S, _ = z3.shape
    tq = ATTN_TQ
    k_col = (2 * D_POOL) // LANES
    g_col = k_col + D_QK // LANES
    vec = lambda n: pl.BlockSpec((1, n), lambda b, h: (0, 0))
    head = lambda col: pl.BlockSpec((None, S, LANES), lambda b, h: (b, 0, col + h))
    head_t = pl.BlockSpec((None, None, LANES, S), lambda b, h: (b, h, 0, 0))
    kern = functools.partial(_attn_kernel, seq=S, tq=tq, lam_init=lam_init)
    return pl.pallas_call(
        kern,
        grid=(B, DIFF_HEADS),
        in_specs=[vec(DIFF_QK_DIM)] * 4 + [vec(DIFF_V_DIM),
                  head_t, head(k_col), head_t, head(g_col)],
        out_specs=pl.BlockSpec((None, S, LANES), lambda b, h: (b, 0, h)),
        out_shape=jax.ShapeDtypeStruct((B, S, D_DIFF), BF16),
        scratch_shapes=[pltpu.VMEM((DIFF_V_DIM + ONES_ROWS, S), BF16),
                        pltpu.VMEM((S // tq, LANES, 2 * tq), BF16),
                        pltpu.VMEM((2, DIFF_V_DIM + ONES_ROWS, 2 * tq), F32)],
        compiler_params=pltpu.CompilerParams(
            dimension_semantics=("parallel", "parallel"), vmem_limit_bytes=VMEM_LIMIT),
        name="diff_attention",
    )(lq1.reshape(1, -1), lk1.reshape(1, -1), lq2.reshape(1, -1), lk2.reshape(1, -1),
      subln_g.reshape(1, -1), q_t, z3, v_t, z3)


def _pool_bands(tm):
    lag = np.arange(tm)[:, None] - np.arange(tm)[None, :]
    hlag = np.arange(POOL_HALO)[:, None] + POOL_HALO - np.arange(POOL_HALO)[None, :]
    band = np.stack([(lag >= 0) & (lag < w) for w in POOL_WINDOWS]).astype(np.float32)
    hband = np.stack([hlag < w for w in POOL_WINDOWS]).astype(np.float32)
    return jnp.asarray(band, BF16), jnp.asarray(hband, BF16)


def _out_proj_kernel(u0_ref, gp0_ref, un_ref, halo_ref, gpn_ref, d_ref, band_ref, hband_ref, wp_ref,
                     ps_ref, w_ref, x_ref, gate_ref, gpost_ref, o_ref, mix_ref, y_ref, *, per_seq):
    i = pl.program_id(0)
    tm = band_ref.shape[1]
    gd = POOL_GROUP_DIM
    n_chunks = len(POOL_WINDOWS)
    cw = w_ref.shape[1] // n_chunks
    part = tm // n_chunks

    def pool_sums(u_ref_, g, halo):
        cols = slice(g * gd, (g + 1) * gd)
        win = jnp.dot(band_ref[g], u_ref_[:, cols], preferred_element_type=F32)
        if halo is None:
            return win
        halo_rows, seq_start = halo
        hwin = jnp.dot(hband_ref[g], halo_rows[:, cols], preferred_element_type=F32)
        hwin = jnp.where(seq_start, 0.0, hwin)
        return jnp.concatenate([win[:POOL_HALO] + hwin, win[POOL_HALO:]], axis=0)

    def pool_mix(u_ref_, gp_ref_, g, win, pos0, slot):
        cols = slice(g * gd, (g + 1) * gd)
        t1 = pos0 + lax.broadcasted_iota(jnp.int32, (tm, 1), 0) + 1
        count = jnp.minimum(t1, POOL_WINDOWS[g]).astype(F32)
        pooled = win / count - u_ref_[:, cols].astype(F32)
        mixed = (jnp.dot(pooled.astype(BF16), wp_ref[g], preferred_element_type=F32)
                 * ps_ref[:, cols])
        mix_ref[slot, :, cols] = (mixed * _silu(gp_ref_[:, cols].astype(F32))).astype(BF16)

    def epilogue_part(slot, n):
        rows = slice(n * part, (n + 1) * part)
        yq = [y_ref[slot, rows, c * cw:(c + 1) * cw] for c in range(n_chunks)]
        ms = sum(jnp.sum(v * v, axis=-1, keepdims=True) for v in yq) / w_ref.shape[1]
        scale = lax.rsqrt(ms + EPS)
        for c, v in enumerate(yq):
            cols = slice(c * cw, (c + 1) * cw)
            yn = v * scale * gpost_ref[:, cols]
            o_ref[rows, cols] = x_ref[rows, cols] + (1.0 + gate_ref[0, :, cols]) * yn

    @pl.when(i == 0)
    def _():
        for g in range(n_chunks):
            pool_mix(u0_ref, gp0_ref, g, pool_sums(u0_ref, g, None), 0, 0)
        y_ref[1] = jnp.zeros(y_ref.shape[1:], F32)

    def step(cur):
        nxt = 1 - cur
        it_next = (i + 1) % per_seq
        halo = (halo_ref, it_next == 0)
        mix_ref[cur, :, D_POOL:] = d_ref[...]
        def chunk(n):
            cols = slice(n * cw, (n + 1) * cw)
            y_ref[cur, :, cols] = jnp.dot(mix_ref[cur], w_ref[:, cols], preferred_element_type=F32)

        def mix(g, win):
            pool_mix(un_ref, gpn_ref, g, win, it_next * tm, nxt)

        win = {g: pool_sums(un_ref, g, halo) for g in (0, 1)}
        chunk(0)
        epilogue_part(nxt, 0)
        mix(0, win[0])
        epilogue_part(nxt, 1)
        win[2] = pool_sums(un_ref, 2, halo)
        chunk(1)
        mix(1, win[1])
        epilogue_part(nxt, 2)
        win[3] = pool_sums(un_ref, 3, halo)
        chunk(2)
        mix(2, win[2])
        epilogue_part(nxt, 3)
        mix(3, win[3])
        chunk(3)

    for parity in range(2):
        pl.when(i % 2 == parity)(functools.partial(step, parity))


def _out_proj(z2, diff_out, w_pool_bf, pool_scale, w_out_bf, x2, mod3, g_post, *, seq, layer):
    M, D = x2.shape
    tm = 256
    per_seq = seq // tm
    n_tiles = M // tm
    band, hband = _pool_bands(tm)
    const = lambda shape: pl.BlockSpec(shape, lambda i: (0,) * len(shape))
    of_layer = lambda shape, **kw: pl.BlockSpec((None,) + shape,
                                                lambda i: (layer,) + (0,) * len(shape), **kw)
    cur = lambda i: jnp.minimum(i, n_tiles - 1)
    nxt = lambda i: jnp.minimum(i + 1, n_tiles - 1)
    prv = lambda i: jnp.maximum(i - 1, 0)
    kern = functools.partial(_out_proj_kernel, per_seq=per_seq)
    return pl.pallas_call(
        kern,
        grid=(n_tiles + 1,),
        in_specs=[
            pl.BlockSpec((tm, D_POOL), lambda i: (0, 0)),
            pl.BlockSpec((tm, D_POOL), lambda i: (0, 1)),
            pl.BlockSpec((tm, D_POOL), lambda i: (nxt(i), 0)),
            pl.BlockSpec((POOL_HALO, D_POOL),
                         lambda i: (nxt(i) * (tm // POOL_HALO) - 1, 0)),
            pl.BlockSpec((tm, D_POOL), lambda i: (nxt(i), 1)),
            pl.BlockSpec((tm, D_DIFF), lambda i: (cur(i), 0)),
            const(band.shape), const(hband.shape), of_layer(w_pool_bf.shape[1:]),
            const((1, D_POOL)), of_layer((D, D), pipeline_mode=pl.Buffered(1)),
            pl.BlockSpec((tm, D), lambda i: (prv(i), 0)),
            pl.BlockSpec((1, 1, D), lambda i: (prv(i) // per_seq, 0, 2)),
            const((1, D)),
        ],
        out_specs=pl.BlockSpec((tm, D), lambda i: (prv(i), 0)),
        out_shape=jax.ShapeDtypeStruct((M, D), F32),
        scratch_shapes=[pltpu.VMEM((2, tm, D), BF16),
                        pltpu.VMEM((2, tm, D), F32)],
        compiler_params=pltpu.CompilerParams(
            dimension_semantics=("arbitrary",), vmem_limit_bytes=VMEM_LIMIT),
        name="out_proj",
    )(z2, z2, z2, z2, z2, diff_out, band, hband, w_pool_bf, pool_scale.reshape(1, D_POOL),
      w_out_bf, x2, mod3, g_post.reshape(1, D))


def _w_in_prep_kernel(w_ref, o_ref, *, tn):
    j = pl.program_id(1)
    first_qk = (2 * D_POOL) // tn
    is_qk = jnp.logical_and(j >= first_qk, j < first_qk + (2 * D_QK) // tn)

    @pl.when(is_qk)
    def _():
        quarter = DIFF_QK_DIM // 2
        blk = lax.broadcasted_iota(jnp.int32, (w_ref.shape[0], LANES), 1) // quarter
        for g in range(tn // LANES):
            x = w_ref[:, g * LANES:(g + 1) * LANES]
            y = jnp.where(blk == 1, pltpu.roll(x, LANES - quarter, 1),
                          jnp.where(blk == 2, pltpu.roll(x, quarter, 1), x))
            o_ref[:, g * LANES:(g + 1) * LANES] = y.astype(BF16)

    @pl.when(jnp.logical_not(is_qk))
    def _():
        o_ref[...] = w_ref[...].astype(BF16)


def _w_in_prep(w_in):
    depth, d, n = w_in.shape
    tn = 512
    return pl.pallas_call(
        functools.partial(_w_in_prep_kernel, tn=tn),
        grid=(depth, n // tn),
        in_specs=[pl.BlockSpec((None, d, tn), lambda l, j: (l, 0, j))],
        out_specs=pl.BlockSpec((None, d, tn), lambda l, j: (l, 0, j)),
        out_shape=jax.ShapeDtypeStruct((depth, d, n), BF16),
        compiler_params=pltpu.CompilerParams(
            dimension_semantics=("parallel", "parallel"), vmem_limit_bytes=VMEM_LIMIT),
        name="w_in_prep",
    )(w_in)


def kernel(x, c, positions, w_ada, b_ada, g_pre, w_in, w_pool, pool_scale, lambda_q1, lambda_k1,
           lambda_q2, lambda_k2, subln_g, w_out, g_post):
    B, S, D = x.shape
    depth = w_in.shape[0]
    cos, sin = _rope_tables(positions)
    mod = _ada_modulation(c, w_ada, b_ada)
    w_in_bf = _w_in_prep(w_in)
    w_pool_bf = w_pool.astype(BF16)
    w_out_bf = w_out.astype(BF16)
    x2 = x.reshape(B * S, D)
    for l in range(depth):
        lam_init = 0.8 - 0.6 * math.exp(-0.3 * l)
        mod3 = mod[l].reshape(B, 1, 3 * D)
        z2, q_t, v_t = _in_proj(x2, mod3, g_pre[l], w_in_bf, cos, sin, seq=S, layer=l)
        diff_out = _diff_attention(z2.reshape(B, S, D_Z), q_t, v_t, lambda_q1[l], lambda_k1[l],
                                   lambda_q2[l], lambda_k2[l], subln_g[l], lam_init=lam_init)
        x2 = _out_proj(z2, diff_out.reshape(B * S, D_DIFF), w_pool_bf, pool_scale[l],
                       w_out_bf, x2, mod3, g_post[l], seq=S, layer=l)
    return x2.reshape(B, S, D)
```

```python
import functools
import math

import jax
import jax.numpy as jnp
from jax import lax
from jax.experimental import pallas as pl
from jax.experimental.pallas import tpu as pltpu
import numpy as np

D_MODEL = 2048
CHUNK = 64
D_POOL = 1024
D_DIFF = 1024
POOL_WINDOWS = (2, 4, 8, 16)
N_POOL_GROUPS = 4
POOL_GROUP_DIM = 256
DIFF_HEADS = 8
DIFF_V_DIM = 128
DIFF_QK_DIM = 64
D_QK = 1024
D_IN = 6144
D_Z = 2 * D_POOL + D_QK + D_DIFF
ROPE_THETA = 10000.0
EPS = 1e-6
NEG_INF = -1e30

LANES = 128
ONES_ROWS = 16
ATTN_TQ = 512
ATTN_GROUP = 256
SCORE_LOOKAHEAD = 7
POOL_HALO = 16
VMEM_LIMIT = 56 * 1024 * 1024
LOG2_E = math.log2(math.e)

F32 = jnp.float32
BF16 = jnp.bfloat16


def _silu(g):
    return g * jax.nn.sigmoid(g)


def _rope_table_kernel(pos_ref, freq_ref, sign_ref, cos_ref, sin_ref):
    ang = pos_ref[0].astype(F32) * freq_ref[...]
    cos_ref[0] = jnp.cos(ang)
    sin_ref[0] = jnp.sin(ang) * sign_ref[...]


def _rope_tables(positions):
    B, S = positions.shape
    ts = 512
    inv_freq = 1.0 / (ROPE_THETA ** (jnp.arange(0, DIFF_QK_DIM, 2, dtype=F32) / DIFF_QK_DIM))
    freq = jnp.tile(inv_freq, LANES // (DIFF_QK_DIM // 2))[None, :]
    sign = jnp.where(jnp.arange(LANES) < LANES // 2, -1.0, 1.0).astype(F32)[None, :]
    cos, sin = pl.pallas_call(
        _rope_table_kernel,
        grid=(B, S // ts),
        in_specs=[
            pl.BlockSpec((1, ts, 1), lambda b, i: (b, i, 0)),
            pl.BlockSpec((1, LANES), lambda b, i: (0, 0)),
            pl.BlockSpec((1, LANES), lambda b, i: (0, 0)),
        ],
        out_specs=[
            pl.BlockSpec((1, ts, LANES), lambda b, i: (b, i, 0)),
            pl.BlockSpec((1, ts, LANES), lambda b, i: (b, i, 0)),
        ],
        out_shape=[jax.ShapeDtypeStruct((B, S, LANES), F32)] * 2,
        name="rope_tables",
    )(positions.reshape(B, S, 1), freq, sign)
    return cos.reshape(B * S, LANES), sin.reshape(B * S, LANES)


def _ada_kernel(c_ref, w_ref, b_ref, o_ref):
    o_ref[0] = jnp.dot(c_ref[...], w_ref[0], preferred_element_type=F32) + b_ref[0]


def _ada_modulation(c, w_ada, b_ada):
    depth, d, n = w_ada.shape
    B = c.shape[0]
    rows = 8
    c_pad = jnp.zeros((rows, d), F32).at[:B].set(c)
    tn = 1024
    out = pl.pallas_call(
        _ada_kernel,
        grid=(depth, n // tn),
        in_specs=[
            pl.BlockSpec((rows, d), lambda l, j: (0, 0)),
            pl.BlockSpec((1, d, tn), lambda l, j: (l, 0, j)),
            pl.BlockSpec((1, 1, tn), lambda l, j: (l, 0, j)),
        ],
        out_specs=pl.BlockSpec((1, rows, tn), lambda l, j: (l, 0, j)),
        out_shape=jax.ShapeDtypeStruct((depth, rows, n), F32),
        compiler_params=pltpu.CompilerParams(vmem_limit_bytes=VMEM_LIMIT),
        name="ada_modulation",
    )(c_pad, w_ada, b_ada.reshape(depth, 1, n))
    return out[:, :B]


def _in_proj_kernel(x0_ref, xn_ref, shift0_ref, scale0_ref, shiftn_ref, scalen_ref, gpre_ref,
                    w_ref, cos_ref, sin_ref, z_ref, qt_ref, vt_ref, h_ref, *, tn):
    i = pl.program_id(0)
    n_col_tiles = w_ref.shape[1] // tn
    q_lo, k_lo, v_lo, gd_lo = (2 * D_POOL, 2 * D_POOL + D_QK, 2 * D_POOL + 2 * D_QK,
                               2 * D_POOL + 2 * D_QK + D_DIFF)

    def normed(x_ref, shift_ref, scale_ref):
        x = x_ref[...]
        ms = jnp.mean(x * x, axis=-1, keepdims=True)
        y = x * lax.rsqrt(ms + EPS) * gpre_ref[...]
        return (y * (1.0 + scale_ref[0]) + shift_ref[0]).astype(BF16)

    @pl.when(i == 0)
    def _():
        h_ref[0] = normed(x0_ref, shift0_ref, scale0_ref)

    def step(cur):
        h_ref[1 - cur] = normed(xn_ref, shiftn_ref, scalen_ref)
        h = h_ref[cur]
        z_col = 0
        for jt in range(n_col_tiles):
            col0 = jt * tn
            acc = jnp.dot(h, w_ref[:, col0:col0 + tn], preferred_element_type=F32)
            is_q, is_k, is_v = (q_lo <= col0 < k_lo), (k_lo <= col0 < v_lo), (v_lo <= col0 < gd_lo)
            if is_q or is_k:
                qk_scale = LOG2_E / math.sqrt(DIFF_QK_DIM) if is_q else 1.0
                cos = cos_ref[...] * qk_scale
                sin = sin_ref[...] * qk_scale
                for g in range(tn // LANES):
                    xg = acc[:, g * LANES:(g + 1) * LANES]
                    rot = pltpu.roll(xg, LANES // 2, 1)
                    roped = xg * cos + rot * sin
                    if is_q:
                        qt_ref[(col0 - q_lo) // LANES + g] = roped.T.astype(BF16)
                    else:
                        z_ref[:, z_col + g * LANES:z_col + (g + 1) * LANES] = roped.astype(BF16)
            elif is_v:
                for g in range(tn // LANES):
                    vt_ref[(col0 - v_lo) // LANES + g] = acc[:, g * LANES:(g + 1) * LANES].T.astype(BF16)
            else:
                z_ref[:, z_col:z_col + tn] = acc.astype(BF16)
            if not (is_q or is_v):
                z_col += tn

    for parity in range(2):
        pl.when(i % 2 == parity)(functools.partial(step, parity))


def _in_proj(x2, mod3, g_pre, w_in_bf, cos, sin, *, seq, layer):
    M, D = x2.shape
    N = w_in_bf.shape[2]
    tm, tn = 256, 1024
    per_seq = seq // tm
    last = M // tm - 1
    nxt = lambda i: jnp.minimum(i + 1, last)
    kern = functools.partial(_in_proj_kernel, tn=tn)
    return pl.pallas_call(
        kern,
        grid=(M // tm,),
        in_specs=[
            pl.BlockSpec((tm, D), lambda i: (0, 0)),
            pl.BlockSpec((tm, D), lambda i: (nxt(i), 0)),
            pl.BlockSpec((1, 1, D), lambda i: (0, 0, 0)),
            pl.BlockSpec((1, 1, D), lambda i: (0, 0, 1)),
            pl.BlockSpec((1, 1, D), lambda i: (nxt(i) // per_seq, 0, 0)),
            pl.BlockSpec((1, 1, D), lambda i: (nxt(i) // per_seq, 0, 1)),
            pl.BlockSpec((1, D), lambda i: (0, 0)),
            pl.BlockSpec((None, D, N), lambda i: (layer, 0, 0), pipeline_mode=pl.Buffered(1)),
            pl.BlockSpec((tm, LANES), lambda i: (i, 0)),
            pl.BlockSpec((tm, LANES), lambda i: (i, 0)),
        ],
        out_specs=[
            pl.BlockSpec((tm, D_Z), lambda i: (i, 0)),
            pl.BlockSpec((None, DIFF_HEADS, LANES, tm), lambda i: (i // per_seq, 0, 0, i % per_seq)),
            pl.BlockSpec((None, DIFF_HEADS, LANES, tm), lambda i: (i // per_seq, 0, 0, i % per_seq)),
        ],
        out_shape=[
            jax.ShapeDtypeStruct((M, D_Z), BF16),
            jax.ShapeDtypeStruct((M // seq, DIFF_HEADS, LANES, seq), BF16),
            jax.ShapeDtypeStruct((M // seq, DIFF_HEADS, LANES, seq), BF16),
        ],
        scratch_shapes=[pltpu.VMEM((2, tm, D), BF16)],
        compiler_params=pltpu.CompilerParams(
            dimension_semantics=("arbitrary",), vmem_limit_bytes=VMEM_LIMIT),
        name="in_proj",
    )(x2, x2, mod3, mod3, mod3, mod3, g_pre.reshape(1, D), w_in_bf, cos, sin)


def _attn_kernel(lq1_ref, lk1_ref, lq2_ref, lk2_ref, subg_ref, q_ref, k_ref, v_ref, g_ref,
                 o_ref, vt_ref, qcat_ref, acc_ref, *, seq, tq, lam_init):
    hq, nq = ATTN_GROUP, seq // tq
    n_parts = tq // hq
    n_groups = 2 * n_parts
    lam = (jnp.exp(jnp.sum(lq1_ref[...] * lk1_ref[...], axis=-1, keepdims=True))
           - jnp.exp(jnp.sum(lq2_ref[...] * lk2_ref[...], axis=-1, keepdims=True))
           + lam_init)

    dim = lax.broadcasted_iota(jnp.int32, (LANES, tq), 0)
    first_map = (dim // (DIFF_QK_DIM // 2)) % 2 == 0
    for c in range(nq):
        cols = slice(c * tq, (c + 1) * tq)
        vt_ref[:DIFF_V_DIM, cols] = v_ref[:, cols]
        q = q_ref[:, cols]
        zero = jnp.zeros_like(q)
        maps = (jnp.where(first_map, q, zero), jnp.where(first_map, zero, q))
        qcat_ref[c] = jnp.concatenate(
            [maps[g % 2][:, (g // 2) * hq:(g // 2 + 1) * hq] for g in range(n_groups)], axis=1)
    vt_ref[DIFF_V_DIM:, :] = jnp.ones((ONES_ROWS, seq), BF16)

    key_c = lax.broadcasted_iota(jnp.int32, (hq, hq), 0) // CHUNK
    row_c = lax.broadcasted_iota(jnp.int32, (hq, hq), 1) // CHUNK
    causal = key_c <= row_c

    def scores(tile):
        _, i, k0, c = tile
        return jnp.dot(k_ref[k0:k0 + hq, :], qcat_ref[i, :, c * hq:(c + 1) * hq],
                       preferred_element_type=F32)

    def softmax_pv(tile, s, m):
        masked, i, k0, c = tile
        acc = acc_ref.at[i % 2]
        cols = slice(c * hq, (c + 1) * hq)
        if masked:
            s = jnp.where(causal, s, NEG_INF)
        m_new = jnp.max(s, axis=0, keepdims=True)
        if m is not None:
            m_new = jnp.maximum(m, m_new)
        p = jnp.exp2(s - m_new).astype(BF16)
        pv = jnp.dot(vt_ref[:, k0:k0 + hq], p, preferred_element_type=F32)
        if m is None:
            acc[:, cols] = pv
        else:
            acc[:, cols] = acc[:, cols] * jnp.exp2(m - m_new) + pv
        return m_new

    def finish(i):
        acc = acc_ref.at[i % 2]
        o = acc[:DIFF_V_DIM, :] / acc[DIFF_V_DIM:DIFF_V_DIM + 1, :]
        by_map = [jnp.concatenate([o[:, (2 * r + mp) * hq:(2 * r + mp + 1) * hq]
                                   for r in range(n_parts)], axis=1) for mp in range(2)]
        att = (by_map[0] - lam * by_map[1]).T
        ms = jnp.mean(att * att, axis=-1, keepdims=True)
        y = att * lax.rsqrt(ms + EPS) * subg_ref[...] * (1.0 - lam_init)
        rows = slice(i * tq, (i + 1) * tq)
        o_ref[rows, :] = (y * _silu(g_ref[rows, :].astype(F32))).astype(BF16)

    tiles = []
    for i in range(nq):
        for k0 in range(0, i * tq, hq):
            tiles += [(False, i, k0, c) for c in range(n_groups)]
        for t in range(n_parts):
            tiles += [(c // 2 == t, i, i * tq + t * hq, c) for c in range(2 * t, n_groups)]
    pending = [scores(t) for t in tiles[:SCORE_LOOKAHEAD]]
    m = {}
    for n, tile in enumerate(tiles):
        if n + SCORE_LOOKAHEAD < len(tiles):
            pending.append(scores(tiles[n + SCORE_LOOKAHEAD]))
        _, i, _, c = tile
        m[c] = softmax_pv(tile, pending.pop(0), m.get(c))
        if n + 1 == len(tiles) or tiles[n + 1][1] != i:
            finish(i)
            m = {}


def _diff_attention(z3, q_t, v_t, lq1, lk1, lq2, lk2, subln_g, *, lam_init):
    B, S, _ = z3.shape
    tq = ATTN_TQ
    k_col = (2 * D_POOL) // LANES
    g_col = k_col + D_QK // LANES
    vec = lambda n: pl.BlockSpec((1, n), lambda b, h: (0, 0))
    head = lambda col: pl.BlockSpec((None, S, LANES), lambda b, h: (b, 0, col + h))
    head_t = pl.BlockSpec((None, None, LANES, S), lambda b, h: (b, h, 0, 0))
    kern = functools.partial(_attn_kernel, seq=S, tq=tq, lam_init=lam_init)
    return pl.pallas_call(
        kern,
        grid=(B, DIFF_HEADS),
        in_specs=[vec(DIFF_QK_DIM)] * 4 + [vec(DIFF_V_DIM),
                  head_t, head(k_col), head_t, head(g_col)],
        out_specs=pl.BlockSpec((None, S, LANES), lambda b, h: (b, 0, h)),
        out_shape=jax.ShapeDtypeStruct((B, S, D_DIFF), BF16),
        scratch_shapes=[pltpu.VMEM((DIFF_V_DIM + ONES_ROWS, S), BF16),
                        pltpu.VMEM((S // tq, LANES, 2 * tq), BF16),
                        pltpu.VMEM((2, DIFF_V_DIM + ONES_ROWS, 2 * tq), F32)],
        compiler_params=pltpu.CompilerParams(
            dimension_semantics=("parallel", "parallel"), vmem_limit_bytes=VMEM_LIMIT),
        name="diff_attention",
    )(lq1.reshape(1, -1), lk1.reshape(1, -1), lq2.reshape(1, -1), lk2.reshape(1, -1),
      subln_g.reshape(1, -1), q_t, z3, v_t, z3)


def _pool_bands(tm):
    lag = np.arange(tm)[:, None] - np.arange(tm)[None, :]
    hlag = np.arange(POOL_HALO)[:, None] + POOL_HALO - np.arange(POOL_HALO)[None, :]
    band = np.stack([(lag >= 0) & (lag < w) for w in POOL_WINDOWS]).astype(np.float32)
    hband = np.stack([hlag < w for w in POOL_WINDOWS]).astype(np.float32)
    return jnp.asarray(band, BF16), jnp.asarray(hband, BF16)


def _out_proj_kernel(u0_ref, gp0_ref, un_ref, halo_ref, gpn_ref, d_ref, band_ref, hband_ref, wp_ref,
                     ps_ref, w_ref, x_ref, gate_ref, gpost_ref, o_ref, mix_ref, y_ref, *, per_seq):
    i = pl.program_id(0)
    tm = band_ref.shape[1]
    gd = POOL_GROUP_DIM
    n_chunks = len(POOL_WINDOWS)
    cw = w_ref.shape[1] // n_chunks
    part = tm // n_chunks

    def pool_sums(u_ref_, g, halo):
        cols = slice(g * gd, (g + 1) * gd)
        win = jnp.dot(band_ref[g], u_ref_[:, cols], preferred_element_type=F32)
        if halo is None:
            return win
        halo_rows, seq_start = halo
        hwin = jnp.dot(hband_ref[g], halo_rows[:, cols], preferred_element_type=F32)
        hwin = jnp.where(seq_start, 0.0, hwin)
        return jnp.concatenate([win[:POOL_HALO] + hwin, win[POOL_HALO:]], axis=0)

    def pool_mix(u_ref_, gp_ref_, g, win, pos0, slot):
        cols = slice(g * gd, (g + 1) * gd)
        t1 = pos0 + lax.broadcasted_iota(jnp.int32, (tm, 1), 0) + 1
        count = jnp.minimum(t1, POOL_WINDOWS[g]).astype(F32)
        pooled = win / count - u_ref_[:, cols].astype(F32)
        mixed = (jnp.dot(pooled.astype(BF16), wp_ref[g], preferred_element_type=F32)
                 * ps_ref[:, cols])
        mix_ref[slot, :, cols] = (mixed * _silu(gp_ref_[:, cols].astype(F32))).astype(BF16)

    def epilogue_part(slot, n):
        rows = slice(n * part, (n + 1) * part)
        yq = [y_ref[slot, rows, c * cw:(c + 1) * cw] for c in range(n_chunks)]
        ms = sum(jnp.sum(v * v, axis=-1, keepdims=True) for v in yq) / w_ref.shape[1]
        scale = lax.rsqrt(ms + EPS)
        for c, v in enumerate(yq):
            cols = slice(c * cw, (c + 1) * cw)
            yn = v * scale * gpost_ref[:, cols]
            o_ref[rows, cols] = x_ref[rows, cols] + (1.0 + gate_ref[0, :, cols]) * yn

    @pl.when(i == 0)
    def _():
        for g in range(n_chunks):
            pool_mix(u0_ref, gp0_ref, g, pool_sums(u0_ref, g, None), 0, 0)
        y_ref[1] = jnp.zeros(y_ref.shape[1:], F32)

    def step(cur):
        nxt = 1 - cur
        it_next = (i + 1) % per_seq
        halo = (halo_ref, it_next == 0)
        mix_ref[cur, :, D_POOL:] = d_ref[...]
        def chunk(n):
            cols = slice(n * cw, (n + 1) * cw)
            y_ref[cur, :, cols] = jnp.dot(mix_ref[cur], w_ref[:, cols], preferred_element_type=F32)

        def mix(g, win):
            pool_mix(un_ref, gpn_ref, g, win, it_next * tm, nxt)

        win = {g: pool_sums(un_ref, g, halo) for g in (0, 1)}
        chunk(0)
        epilogue_part(nxt, 0)
        mix(0, win[0])
        epilogue_part(nxt, 1)
        win[2] = pool_sums(un_ref, 2, halo)
        chunk(1)
        mix(1, win[1])
        epilogue_part(nxt, 2)
        win[3] = pool_sums(un_ref, 3, halo)
        chunk(2)
        mix(2, win[2])
        epilogue_part(nxt, 3)
        mix(3, win[3])
        chunk(3)

    for parity in range(2):
        pl.when(i % 2 == parity)(functools.partial(step, parity))


def _out_proj(z2, diff_out, w_pool_bf, pool_scale, w_out_bf, x2, mod3, g_post, *, seq, layer):
    M, D = x2.shape
    tm = 256
    per_seq = seq // tm
    n_tiles = M // tm
    band, hband = _pool_bands(tm)
    const = lambda shape: pl.BlockSpec(shape, lambda i: (0,) * len(shape))
    of_layer = lambda shape, **kw: pl.BlockSpec((None,) + shape,
                                                lambda i: (layer,) + (0,) * len(shape), **kw)
    cur = lambda i: jnp.minimum(i, n_tiles - 1)
    nxt = lambda i: jnp.minimum(i + 1, n_tiles - 1)
    prv = lambda i: jnp.maximum(i - 1, 0)
    kern = functools.partial(_out_proj_kernel, per_seq=per_seq)
    return pl.pallas_call(
        kern,
        grid=(n_tiles + 1,),
        in_specs=[
            pl.BlockSpec((tm, D_POOL), lambda i: (0, 0)),
            pl.BlockSpec((tm, D_POOL), lambda i: (0, 1)),
            pl.BlockSpec((tm, D_POOL), lambda i: (nxt(i), 0)),
            pl.BlockSpec((POOL_HALO, D_POOL),
                         lambda i: (nxt(i) * (tm // POOL_HALO) - 1, 0)),
            pl.BlockSpec((tm, D_POOL), lambda i: (nxt(i), 1)),
            pl.BlockSpec((tm, D_DIFF), lambda i: (cur(i), 0)),
            const(band.shape), const(hband.shape), of_layer(w_pool_bf.shape[1:]),
            const((1, D_POOL)), of_layer((D, D), pipeline_mode=pl.Buffered(1)),
            pl.BlockSpec((tm, D), lambda i: (prv(i), 0)),
            pl.BlockSpec((1, 1, D), lambda i: (prv(i) // per_seq, 0, 2)),
            const((1, D)),
        ],
        out_specs=pl.BlockSpec((tm, D), lambda i: (prv(i), 0)),
        out_shape=jax.ShapeDtypeStruct((M, D), F32),
        scratch_shapes=[pltpu.VMEM((2, tm, D), BF16),
                        pltpu.VMEM((2, tm, D), F32)],
        compiler_params=pltpu.CompilerParams(
            dimension_semantics=("arbitrary",), vmem_limit_bytes=VMEM_LIMIT),
        name="out_proj",
    )(z2, z2, z2, z2, z2, diff_out, band, hband, w_pool_bf, pool_scale.reshape(1, D_POOL),
      w_out_bf, x2, mod3, g_post.reshape(1, D))


def _w_in_prep_kernel(w_ref, o_ref, *, tn):
    j = pl.program_id(1)
    first_qk = (2 * D_POOL) // tn
    is_qk = jnp.logical_and(j >= first_qk, j < first_qk + (2 * D_QK) // tn)

    @pl.when(is_qk)
    def _():
        quarter = DIFF_QK_DIM // 2
        blk = lax.broadcasted_iota(jnp.int32, (w_ref.shape[0], LANES), 1) // quarter
        for g in range(tn // LANES):
            x = w_ref[:, g * LANES:(g + 1) * LANES]
            y = jnp.where(blk == 1, pltpu.roll(x, LANES - quarter, 1),
                          jnp.where(blk == 2, pltpu.roll(x, quarter, 1), x))
            o_ref[:, g * LANES:(g + 1) * LANES] = y.astype(BF16)

    @pl.when(jnp.logical_not(is_qk))
    def _():
        o_ref[...] = w_ref[...].astype(BF16)


def _w_in_prep(w_in):
    depth, d, n = w_in.shape
    tn = 512
    return pl.pallas_call(
        functools.partial(_w_in_prep_kernel, tn=tn),
        grid=(depth, n // tn),
        in_specs=[pl.BlockSpec((None, d, tn), lambda l, j: (l, 0, j))],
        out_specs=pl.BlockSpec((None, d, tn), lambda l, j: (l, 0, j)),
        out_shape=jax.ShapeDtypeStruct((depth, d, n), BF16),
        compiler_params=pltpu.CompilerParams(
            dimension_semantics=("parallel", "parallel"), vmem_limit_bytes=VMEM_LIMIT),
        name="w_in_prep",
    )(w_in)


def kernel(x, c, positions, w_ada, b_ada, g_pre, w_in, w_pool, pool_scale, lambda_q1, lambda_k1,
           lambda_q2, lambda_k2, subln_g, w_out, g_post):
    B, S, D = x.shape
    depth = w_in.shape[0]
    cos, sin = _rope_tables(positions)
    mod = _ada_modulation(c, w_ada, b_ada)
    w_in_bf = _w_in_prep(w_in)
    w_pool_bf = w_pool.astype(BF16)
    w_out_bf = w_out.astype(BF16)
    x2 = x.reshape(B * S, D)
    for l in range(depth):
        lam_init = 0.8 - 0.6 * math.exp(-0.3 * l)
        mod3 = mod[l].reshape(B, 1, 3 * D)
        z2, q_t, v_t = _in_proj(x2, mod3, g_pre[l], w_in_bf, cos, sin, seq=S, layer=l)
        diff_out = _diff_attention(z2.reshape(B, S, D_Z), q_t, v_t, lambda_q1[l], lambda_k1[l],
                                   lambda_q2[l], lambda_k2[l], subln_g[l], lam_init=lam_init)
        x2 = _out_proj(z2, diff_out.reshape(B * S, D_DIFF), w_pool_bf, pool_scale[l],
                       w_out_bf, x2, mod3, g_post[l], seq=S, layer=l)
    return x2.reshape(B, S, D)
```

```python
import functools
import math

import jax
import jax.numpy as jnp
from jax import lax
from jax.experimental import pallas as pl
from jax.experimental.pallas import tpu as pltpu
import numpy as np

D_MODEL = 2048
CHUNK = 64
D_POOL = 1024
D_DIFF = 1024
POOL_WINDOWS = (2, 4, 8, 16)
N_POOL_GROUPS = 4
POOL_GROUP_DIM = 256
DIFF_HEADS = 8
DIFF_V_DIM = 128
DIFF_QK_DIM = 64
D_QK = 1024
D_IN = 6144
D_Z = 2 * D_POOL + D_QK + D_DIFF
ROPE_THETA = 10000.0
EPS = 1e-6
NEG_INF = -1e30

LANES = 128
ONES_ROWS = 16
ATTN_TQ = 512
ATTN_GROUP = 256
SCORE_LOOKAHEAD = 7
POOL_HALO = 16
VMEM_LIMIT = 56 * 1024 * 1024
LOG2_E = math.log2(math.e)

F32 = jnp.float32
BF16 = jnp.bfloat16


def _silu(g):
    return g * jax.nn.sigmoid(g)


def _rope_table_kernel(pos_ref, freq_ref, sign_ref, cos_ref, sin_ref):
    ang = pos_ref[0].astype(F32) * freq_ref[...]
    cos_ref[0] = jnp.cos(ang)
    sin_ref[0] = jnp.sin(ang) * sign_ref[...]


def _rope_tables(positions):
    B, S = positions.shape
    ts = 512
    inv_freq = 1.0 / (ROPE_THETA ** (jnp.arange(0, DIFF_QK_DIM, 2, dtype=F32) / DIFF_QK_DIM))
    freq = jnp.tile(inv_freq, LANES // (DIFF_QK_DIM // 2))[None, :]
    sign = jnp.where(jnp.arange(LANES) < LANES // 2, -1.0, 1.0).astype(F32)[None, :]
    cos, sin = pl.pallas_call(
        _rope_table_kernel,
        grid=(B, S // ts),
        in_specs=[
            pl.BlockSpec((1, ts, 1), lambda b, i: (b, i, 0)),
            pl.BlockSpec((1, LANES), lambda b, i: (0, 0)),
            pl.BlockSpec((1, LANES), lambda b, i: (0, 0)),
        ],
        out_specs=[
            pl.BlockSpec((1, ts, LANES), lambda b, i: (b, i, 0)),
            pl.BlockSpec((1, ts, LANES), lambda b, i: (b, i, 0)),
        ],
        out_shape=[jax.ShapeDtypeStruct((B, S, LANES), F32)] * 2,
        name="rope_tables",
    )(positions.reshape(B, S, 1), freq, sign)
    return cos.reshape(B * S, LANES), sin.reshape(B * S, LANES)


def _ada_kernel(c_ref, w_ref, b_ref, o_ref):
    o_ref[0] = jnp.dot(c_ref[...], w_ref[0], preferred_element_type=F32) + b_ref[0]


def _ada_modulation(c, w_ada, b_ada):
    depth, d, n = w_ada.shape
    B = c.shape[0]
    rows = 8
    c_pad = jnp.zeros((rows, d), F32).at[:B].set(c)
    tn = 2048
    out = pl.pallas_call(
        _ada_kernel,
        grid=(depth, n // tn),
        in_specs=[
            pl.BlockSpec((rows, d), lambda l, j: (0, 0)),
            pl.BlockSpec((1, d, tn), lambda l, j: (l, 0, j)),
            pl.BlockSpec((1, 1, tn), lambda l, j: (l, 0, j)),
        ],
        out_specs=pl.BlockSpec((1, rows, tn), lambda l, j: (l, 0, j)),
        out_shape=jax.ShapeDtypeStruct((depth, rows, n), F32),
        compiler_params=pltpu.CompilerParams(vmem_limit_bytes=VMEM_LIMIT),
        name="ada_modulation",
    )(c_pad, w_ada, b_ada.reshape(depth, 1, n))
    return out[:, :B]


def _in_proj_kernel(x0_ref, xn_ref, shift0_ref, scale0_ref, shiftn_ref, scalen_ref, gpre_ref,
                    w_ref, cos_ref, sin_ref, z_ref, qt_ref, vt_ref, h_ref, *, tn):
    i = pl.program_id(0)
    n_col_tiles = w_ref.shape[1] // tn
    q_lo, k_lo, v_lo, gd_lo = (2 * D_POOL, 2 * D_POOL + D_QK, 2 * D_POOL + 2 * D_QK,
                               2 * D_POOL + 2 * D_QK + D_DIFF)

    def normed(x_ref, shift_ref, scale_ref):
        x = x_ref[...]
        ms = jnp.mean(x * x, axis=-1, keepdims=True)
        y = x * lax.rsqrt(ms + EPS) * gpre_ref[...]
        return (y * (1.0 + scale_ref[0]) + shift_ref[0]).astype(BF16)

    @pl.when(i == 0)
    def _():
        h_ref[0] = normed(x0_ref, shift0_ref, scale0_ref)

    def step(cur):
        h_ref[1 - cur] = normed(xn_ref, shiftn_ref, scalen_ref)
        h = h_ref[cur]
        z_col = 0
        for jt in range(n_col_tiles):
            col0 = jt * tn
            acc = jnp.dot(h, w_ref[:, col0:col0 + tn], preferred_element_type=F32)
            is_q, is_k, is_v = (q_lo <= col0 < k_lo), (k_lo <= col0 < v_lo), (v_lo <= col0 < gd_lo)
            if is_q or is_k:
                qk_scale = LOG2_E / math.sqrt(DIFF_QK_DIM) if is_q else 1.0
                cos = cos_ref[...] * qk_scale
                sin = sin_ref[...] * qk_scale
                for g in range(tn // LANES):
                    xg = acc[:, g * LANES:(g + 1) * LANES]
                    rot = pltpu.roll(xg, LANES // 2, 1)
                    roped = xg * cos + rot * sin
                    if is_q:
                        qt_ref[(col0 - q_lo) // LANES + g] = roped.T.astype(BF16)
                    else:
                        z_ref[:, z_col + g * LANES:z_col + (g + 1) * LANES] = roped.astype(BF16)
            elif is_v:
                for g in range(tn // LANES):
                    vt_ref[(col0 - v_lo) // LANES + g] = acc[:, g * LANES:(g + 1) * LANES].T.astype(BF16)
            else:
                z_ref[:, z_col:z_col + tn] = acc.astype(BF16)
            if not (is_q or is_v):
                z_col += tn

    for parity in range(2):
        pl.when(i % 2 == parity)(functools.partial(step, parity))


def _in_proj(x2, mod3, g_pre, w_in_bf, cos, sin, *, seq, layer):
    M, D = x2.shape
    N = w_in_bf.shape[2]
    tm, tn = 256, 1024
    per_seq = seq // tm
    last = M // tm - 1
    nxt = lambda i: jnp.minimum(i + 1, last)
    kern = functools.partial(_in_proj_kernel, tn=tn)
    return pl.pallas_call(
        kern,
        grid=(M // tm,),
        in_specs=[
            pl.BlockSpec((tm, D), lambda i: (0, 0)),
            pl.BlockSpec((tm, D), lambda i: (nxt(i), 0)),
            pl.BlockSpec((1, 1, D), lambda i: (0, 0, 0)),
            pl.BlockSpec((1, 1, D), lambda i: (0, 0, 1)),
            pl.BlockSpec((1, 1, D), lambda i: (nxt(i) // per_seq, 0, 0)),
            pl.BlockSpec((1, 1, D), lambda i: (nxt(i) // per_seq, 0, 1)),
            pl.BlockSpec((1, D), lambda i: (0, 0)),
            pl.BlockSpec((None, D, N), lambda i: (layer, 0, 0), pipeline_mode=pl.Buffered(1)),
            pl.BlockSpec((tm, LANES), lambda i: (i, 0)),
            pl.BlockSpec((tm, LANES), lambda i: (i, 0)),
        ],
        out_specs=[
            pl.BlockSpec((tm, D_Z), lambda i: (i, 0)),
            pl.BlockSpec((None, DIFF_HEADS, LANES, tm), lambda i: (i // per_seq, 0, 0, i % per_seq)),
            pl.BlockSpec((None, DIFF_HEADS, LANES, tm), lambda i: (i // per_seq, 0, 0, i % per_seq)),
        ],
        out_shape=[
            jax.ShapeDtypeStruct((M, D_Z), BF16),
            jax.ShapeDtypeStruct((M // seq, DIFF_HEADS, LANES, seq), BF16),
            jax.ShapeDtypeStruct((M // seq, DIFF_HEADS, LANES, seq), BF16),
        ],
        scratch_shapes=[pltpu.VMEM((2, tm, D), BF16)],
        compiler_params=pltpu.CompilerParams(
            dimension_semantics=("arbitrary",), vmem_limit_bytes=VMEM_LIMIT),
        name="in_proj",
    )(x2, x2, mod3, mod3, mod3, mod3, g_pre.reshape(1, D), w_in_bf, cos, sin)


def _attn_kernel(lq1_ref, lk1_ref, lq2_ref, lk2_ref, subg_ref, q_ref, k_ref, v_ref, g_ref,
                 o_ref, vt_ref, qcat_ref, acc_ref, *, seq, tq, lam_init):
    hq, nq = ATTN_GROUP, seq // tq
    n_parts = tq // hq
    n_groups = 2 * n_parts
    lam = (jnp.exp(jnp.sum(lq1_ref[...] * lk1_ref[...], axis=-1, keepdims=True))
           - jnp.exp(jnp.sum(lq2_ref[...] * lk2_ref[...], axis=-1, keepdims=True))
           + lam_init)

    dim = lax.broadcasted_iota(jnp.int32, (LANES, tq), 0)
    first_map = (dim // (DIFF_QK_DIM // 2)) % 2 == 0
    for c in range(nq):
        cols = slice(c * tq, (c + 1) * tq)
        vt_ref[:DIFF_V_DIM, cols] = v_ref[:, cols]
        q = q_ref[:, cols]
        zero = jnp.zeros_like(q)
        maps = (jnp.where(first_map, q, zero), jnp.where(first_map, zero, q))
        qcat_ref[c] = jnp.concatenate(
            [maps[g % 2][:, (g // 2) * hq:(g // 2 + 1) * hq] for g in range(n_groups)], axis=1)
    vt_ref[DIFF_V_DIM:, :] = jnp.ones((ONES_ROWS, seq), BF16)

    key_c = lax.broadcasted_iota(jnp.int32, (hq, hq), 0) // CHUNK
    row_c = lax.broadcasted_iota(jnp.int32, (hq, hq), 1) // CHUNK
    causal = key_c <= row_c

    def scores(tile):
        _, i, k0, c = tile
        return jnp.dot(k_ref[k0:k0 + hq, :], qcat_ref[i, :, c * hq:(c + 1) * hq],
                       preferred_element_type=F32)

    def softmax_pv(tile, s, m):
        masked, i, k0, c = tile
        acc = acc_ref.at[i % 2]
        cols = slice(c * hq, (c + 1) * hq)
        if masked:
            s = jnp.where(causal, s, NEG_INF)
        m_new = jnp.max(s, axis=0, keepdims=True)
        if m is not None:
            m_new = jnp.maximum(m, m_new)
        p = jnp.exp2(s - m_new).astype(BF16)
        pv = jnp.dot(vt_ref[:, k0:k0 + hq], p, preferred_element_type=F32)
        if m is None:
            acc[:, cols] = pv
        else:
            acc[:, cols] = acc[:, cols] * jnp.exp2(m - m_new) + pv
        return m_new

    def finish(i):
        acc = acc_ref.at[i % 2]
        o = acc[:DIFF_V_DIM, :] / acc[DIFF_V_DIM:DIFF_V_DIM + 1, :]
        by_map = [jnp.concatenate([o[:, (2 * r + mp) * hq:(2 * r + mp + 1) * hq]
                                   for r in range(n_parts)], axis=1) for mp in range(2)]
        att = (by_map[0] - lam * by_map[1]).T
        ms = jnp.mean(att * att, axis=-1, keepdims=True)
        y = att * lax.rsqrt(ms + EPS) * subg_ref[...] * (1.0 - lam_init)
        rows = slice(i * tq, (i + 1) * tq)
        o_ref[rows, :] = (y * _silu(g_ref[rows, :].astype(F32))).astype(BF16)

    tiles = []
    for i in range(nq):
        for k0 in range(0, i * tq, hq):
            tiles += [(False, i, k0, c) for c in range(n_groups)]
        for t in range(n_parts):
            tiles += [(c // 2 == t, i, i * tq + t * hq, c) for c in range(2 * t, n_groups)]
    pending = [scores(t) for t in tiles[:SCORE_LOOKAHEAD]]
    m = {}
    for n, tile in enumerate(tiles):
        if n + SCORE_LOOKAHEAD < len(tiles):
            pending.append(scores(tiles[n + SCORE_LOOKAHEAD]))
        _, i, _, c = tile
        m[c] = softmax_pv(tile, pending.pop(0), m.get(c))
        if n + 1 == len(tiles) or tiles[n + 1][1] != i:
            finish(i)
            m = {}


def _diff_attention(z3, q_t, v_t, lq1, lk1, lq2, lk2, subln_g, *, lam_init):
    B, S, _ = z3.shape
    tq = ATTN_TQ
    k_col = (2 * D_POOL) // LANES
    g_col = k_col + D_QK // LANES
    vec = lambda n: pl.BlockSpec((1, n), lambda b, h: (0, 0))
    head = lambda col: pl.BlockSpec((None, S, LANES), lambda b, h: (b, 0, col + h))
    head_t = pl.BlockSpec((None, None, LANES, S), lambda b, h: (b, h, 0, 0))
    kern = functools.partial(_attn_kernel, seq=S, tq=tq, lam_init=lam_init)
    return pl.pallas_call(
        kern,
        grid=(B, DIFF_HEADS),
        in_specs=[vec(DIFF_QK_DIM)] * 4 + [vec(DIFF_V_DIM),
                  head_t, head(k_col), head_t, head(g_col)],
        out_specs=pl.BlockSpec((None, S, LANES), lambda b, h: (b, 0, h)),
        out_shape=jax.ShapeDtypeStruct((B, S, D_DIFF), BF16),
        scratch_shapes=[pltpu.VMEM((DIFF_V_DIM + ONES_ROWS, S), BF16),
                        pltpu.VMEM((S // tq, LANES, 2 * tq), BF16),
                        pltpu.VMEM((2, DIFF_V_DIM + ONES_ROWS, 2 * tq), F32)],
        compiler_params=pltpu.CompilerParams(
            dimension_semantics=("parallel", "parallel"), vmem_limit_bytes=VMEM_LIMIT),
        name="diff_attention",
    )(lq1.reshape(1, -1), lk1.reshape(1, -1), lq2.reshape(1, -1), lk2.reshape(1, -1),
      subln_g.reshape(1, -1), q_t, z3, v_t, z3)


def _pool_bands(tm):
    lag = np.arange(tm)[:, None] - np.arange(tm)[None, :]
    hlag = np.arange(POOL_HALO)[:, None] + POOL_HALO - np.arange(POOL_HALO)[None, :]
    band = np.stack([(lag >= 0) & (lag < w) for w in POOL_WINDOWS]).astype(np.float32)
    hband = np.stack([hlag < w for w in POOL_WINDOWS]).astype(np.float32)
    return jnp.asarray(band, BF16), jnp.asarray(hband, BF16)


def _out_proj_kernel(u0_ref, gp0_ref, un_ref, halo_ref, gpn_ref, d_ref, band_ref, hband_ref, wp_ref,
                     ps_ref, w_ref, x_ref, gate_ref, gpost_ref, o_ref, mix_ref, y_ref, *, per_seq):
    i = pl.program_id(0)
    tm = band_ref.shape[1]
    gd = POOL_GROUP_DIM
    n_chunks = len(POOL_WINDOWS)
    cw = w_ref.shape[1] // n_chunks
    part = tm // n_chunks

    def pool_sums(u_ref_, g, halo):
        cols = slice(g * gd, (g + 1) * gd)
        win = jnp.dot(band_ref[g], u_ref_[:, cols], preferred_element_type=F32)
        if halo is None:
            return win
        halo_rows, seq_start = halo
        hwin = jnp.dot(hband_ref[g], halo_rows[:, cols], preferred_element_type=F32)
        hwin = jnp.where(seq_start, 0.0, hwin)
        return jnp.concatenate([win[:POOL_HALO] + hwin, win[POOL_HALO:]], axis=0)

    def pool_mix(u_ref_, gp_ref_, g, win, pos0, slot):
        cols = slice(g * gd, (g + 1) * gd)
        t1 = pos0 + lax.broadcasted_iota(jnp.int32, (tm, 1), 0) + 1
        count = jnp.minimum(t1, POOL_WINDOWS[g]).astype(F32)
        pooled = win / count - u_ref_[:, cols].astype(F32)
        mixed = (jnp.dot(pooled.astype(BF16), wp_ref[g], preferred_element_type=F32)
                 * ps_ref[:, cols])
        mix_ref[slot, :, cols] = (mixed * _silu(gp_ref_[:, cols].astype(F32))).astype(BF16)

    def epilogue_part(slot, n):
        rows = slice(n * part, (n + 1) * part)
        yq = [y_ref[slot, rows, c * cw:(c + 1) * cw] for c in range(n_chunks)]
        ms = sum(jnp.sum(v * v, axis=-1, keepdims=True) for v in yq) / w_ref.shape[1]
        scale = lax.rsqrt(ms + EPS)
        for c, v in enumerate(yq):
            cols = slice(c * cw, (c + 1) * cw)
            yn = v * scale * gpost_ref[:, cols]
            o_ref[rows, cols] = x_ref[rows, cols] + (1.0 + gate_ref[0, :, cols]) * yn

    @pl.when(i == 0)
    def _():
        for g in range(n_chunks):
            pool_mix(u0_ref, gp0_ref, g, pool_sums(u0_ref, g, None), 0, 0)
        y_ref[1] = jnp.zeros(y_ref.shape[1:], F32)

    def step(cur):
        nxt = 1 - cur
        it_next = (i + 1) % per_seq
        halo = (halo_ref, it_next == 0)
        mix_ref[cur, :, D_POOL:] = d_ref[...]
        def chunk(n):
            cols = slice(n * cw, (n + 1) * cw)
            y_ref[cur, :, cols] = jnp.dot(mix_ref[cur], w_ref[:, cols], preferred_element_type=F32)

        def mix(g, win):
            pool_mix(un_ref, gpn_ref, g, win, it_next * tm, nxt)

        win = {g: pool_sums(un_ref, g, halo) for g in (0, 1)}
        chunk(0)
        epilogue_part(nxt, 0)
        mix(0, win[0])
        epilogue_part(nxt, 1)
        win[2] = pool_sums(un_ref, 2, halo)
        chunk(1)
        mix(1, win[1])
        epilogue_part(nxt, 2)
        win[3] = pool_sums(un_ref, 3, halo)
        chunk(2)
        mix(2, win[2])
        epilogue_part(nxt, 3)
        mix(3, win[3])
        chunk(3)

    for parity in range(2):
        pl.when(i % 2 == parity)(functools.partial(step, parity))


def _out_proj(z2, diff_out, w_pool_bf, pool_scale, w_out_bf, x2, mod3, g_post, *, seq, layer):
    M, D = x2.shape
    tm = 256
    per_seq = seq // tm
    n_tiles = M // tm
    band, hband = _pool_bands(tm)
    const = lambda shape: pl.BlockSpec(shape, lambda i: (0,) * len(shape))
    of_layer = lambda shape, **kw: pl.BlockSpec((None,) + shape,
                                                lambda i: (layer,) + (0,) * len(shape), **kw)
    cur = lambda i: jnp.minimum(i, n_tiles - 1)
    nxt = lambda i: jnp.minimum(i + 1, n_tiles - 1)
    prv = lambda i: jnp.maximum(i - 1, 0)
    kern = functools.partial(_out_proj_kernel, per_seq=per_seq)
    return pl.pallas_call(
        kern,
        grid=(n_tiles + 1,),
        in_specs=[
            pl.BlockSpec((tm, D_POOL), lambda i: (0, 0)),
            pl.BlockSpec((tm, D_POOL), lambda i: (0, 1)),
            pl.BlockSpec((tm, D_POOL), lambda i: (nxt(i), 0)),
            pl.BlockSpec((POOL_HALO, D_POOL),
                         lambda i: (nxt(i) * (tm // POOL_HALO) - 1, 0)),
            pl.BlockSpec((tm, D_POOL), lambda i: (nxt(i), 1)),
            pl.BlockSpec((tm, D_DIFF), lambda i: (cur(i), 0)),
            const(band.shape), const(hband.shape), of_layer(w_pool_bf.shape[1:]),
            const((1, D_POOL)), of_layer((D, D), pipeline_mode=pl.Buffered(1)),
            pl.BlockSpec((tm, D), lambda i: (prv(i), 0)),
            pl.BlockSpec((1, 1, D), lambda i: (prv(i) // per_seq, 0, 2)),
            const((1, D)),
        ],
        out_specs=pl.BlockSpec((tm, D), lambda i: (prv(i), 0)),
        out_shape=jax.ShapeDtypeStruct((M, D), F32),
        scratch_shapes=[pltpu.VMEM((2, tm, D), BF16),
                        pltpu.VMEM((2, tm, D), F32)],
        compiler_params=pltpu.CompilerParams(
            dimension_semantics=("arbitrary",), vmem_limit_bytes=VMEM_LIMIT),
        name="out_proj",
    )(z2, z2, z2, z2, z2, diff_out, band, hband, w_pool_bf, pool_scale.reshape(1, D_POOL),
      w_out_bf, x2, mod3, g_post.reshape(1, D))


def _w_in_prep_kernel(w_ref, o_ref, *, tn):
    j = pl.program_id(1)
    first_qk = (2 * D_POOL) // tn
    is_qk = jnp.logical_and(j >= first_qk, j < first_qk + (2 * D_QK) // tn)

    @pl.when(is_qk)
    def _():
        quarter = DIFF_QK_DIM // 2
        blk = lax.broadcasted_iota(jnp.int32, (w_ref.shape[0], LANES), 1) // quarter
        for g in range(tn // LANES):
            x = w_ref[:, g * LANES:(g + 1) * LANES]
            y = jnp.where(blk == 1, pltpu.roll(x, LANES - quarter, 1),
                          jnp.where(blk == 2, pltpu.roll(x, quarter, 1), x))
            o_ref[:, g * LANES:(g + 1) * LANES] = y.astype(BF16)

    @pl.when(jnp.logical_not(is_qk))
    def _():
        o_ref[...] = w_ref[...].astype(BF16)


def _w_in_prep(w_in):
    depth, d, n = w_in.shape
    tn = 1024
    return pl.pallas_call(
        functools.partial(_w_in_prep_kernel, tn=tn),
        grid=(depth, n // tn),
        in_specs=[pl.BlockSpec((None, d, tn), lambda l, j: (l, 0, j))],
        out_specs=pl.BlockSpec((None, d, tn), lambda l, j: (l, 0, j)),
        out_shape=jax.ShapeDtypeStruct((depth, d, n), BF16),
        compiler_params=pltpu.CompilerParams(
            dimension_semantics=("parallel", "parallel"), vmem_limit_bytes=VMEM_LIMIT),
        name="w_in_prep",
    )(w_in)


def kernel(x, c, positions, w_ada, b_ada, g_pre, w_in, w_pool, pool_scale, lambda_q1, lambda_k1,
           lambda_q2, lambda_k2, subln_g, w_out, g_post):
    B, S, D = x.shape
    depth = w_in.shape[0]
    cos, sin = _rope_tables(positions)
    mod = _ada_modulation(c, w_ada, b_ada)
    w_in_bf = _w_in_prep(w_in)
    w_pool_bf = w_pool.astype(BF16)
    w_out_bf = w_out.astype(BF16)
    x2 = x.reshape(B * S, D)
    for l in range(depth):
        lam_init = 0.8 - 0.6 * math.exp(-0.3 * l)
        mod3 = mod[l].reshape(B, 1, 3 * D)
        z2, q_t, v_t = _in_proj(x2, mod3, g_pre[l], w_in_bf, cos, sin, seq=S, layer=l)
        diff_out = _diff_attention(z2.reshape(B, S, D_Z), q_t, v_t, lambda_q1[l], lambda_k1[l],
                                   lambda_q2[l], lambda_k2[l], subln_g[l], lam_init=lam_init)
        x2 = _out_proj(z2, diff_out.reshape(B * S, D_DIFF), w_pool_bf, pool_scale[l],
                       w_out_bf, x2, mod3, g_post[l], seq=S, layer=l)
    return x2.reshape(B, S, D)
```

```python
import functools
import math

import jax
import jax.numpy as jnp
from jax import lax
from jax.experimental import pallas as pl
from jax.experimental.pallas import tpu as pltpu
import numpy as np

D_MODEL = 2048
CHUNK = 64
D_POOL = 1024
D_DIFF = 1024
POOL_WINDOWS = (2, 4, 8, 16)
N_POOL_GROUPS = 4
POOL_GROUP_DIM = 256
DIFF_HEADS = 8
DIFF_V_DIM = 128
DIFF_QK_DIM = 64
D_QK = 1024
D_IN = 6144
D_Z = 2 * D_POOL + D_QK + D_DIFF
ROPE_THETA = 10000.0
EPS = 1e-6
NEG_INF = -1e30

LANES = 128
ONES_ROWS = 16
ATTN_TQ = 512
ATTN_GROUP = 256
SCORE_LOOKAHEAD = 7
NORM_ROWS = 16
POOL_HALO = 16
VMEM_LIMIT = 56 * 1024 * 1024
LOG2_E = math.log2(math.e)

F32 = jnp.float32
BF16 = jnp.bfloat16


def _silu(g):
    return g * jax.nn.sigmoid(g)


def _rope_table_kernel(pos_ref, freq_ref, sign_ref, cos_ref, sin_ref):
    ang = pos_ref[0].astype(F32) * freq_ref[...]
    cos_ref[0] = jnp.cos(ang)
    sin_ref[0] = jnp.sin(ang) * sign_ref[...]


def _rope_tables(positions):
    B, S = positions.shape
    ts = 512
    inv_freq = 1.0 / (ROPE_THETA ** (jnp.arange(0, DIFF_QK_DIM, 2, dtype=F32) / DIFF_QK_DIM))
    freq = jnp.tile(inv_freq, LANES // (DIFF_QK_DIM // 2))[None, :]
    sign = jnp.where(jnp.arange(LANES) < LANES // 2, -1.0, 1.0).astype(F32)[None, :]
    cos, sin = pl.pallas_call(
        _rope_table_kernel,
        grid=(B, S // ts),
        in_specs=[
            pl.BlockSpec((1, ts, 1), lambda b, i: (b, i, 0)),
            pl.BlockSpec((1, LANES), lambda b, i: (0, 0)),
            pl.BlockSpec((1, LANES), lambda b, i: (0, 0)),
        ],
        out_specs=[
            pl.BlockSpec((1, ts, LANES), lambda b, i: (b, i, 0)),
            pl.BlockSpec((1, ts, LANES), lambda b, i: (b, i, 0)),
        ],
        out_shape=[jax.ShapeDtypeStruct((B, S, LANES), F32)] * 2,
        name="rope_tables",
    )(positions.reshape(B, S, 1), freq, sign)
    return cos.reshape(B * S, LANES), sin.reshape(B * S, LANES)


def _ada_kernel(c_ref, w_ref, b_ref, o_ref):
    o_ref[0] = jnp.dot(c_ref[...], w_ref[0], preferred_element_type=F32) + b_ref[0]


def _ada_modulation(c, w_ada, b_ada):
    depth, d, n = w_ada.shape
    B = c.shape[0]
    rows = 8
    c_pad = jnp.zeros((rows, d), F32).at[:B].set(c)
    tn = 1024
    out = pl.pallas_call(
        _ada_kernel,
        grid=(depth, n // tn),
        in_specs=[
            pl.BlockSpec((rows, d), lambda l, j: (0, 0)),
            pl.BlockSpec((1, d, tn), lambda l, j: (l, 0, j)),
            pl.BlockSpec((1, 1, tn), lambda l, j: (l, 0, j)),
        ],
        out_specs=pl.BlockSpec((1, rows, tn), lambda l, j: (l, 0, j)),
        out_shape=jax.ShapeDtypeStruct((depth, rows, n), F32),
        compiler_params=pltpu.CompilerParams(vmem_limit_bytes=VMEM_LIMIT),
        name="ada_modulation",
    )(c_pad, w_ada, b_ada.reshape(depth, 1, n))
    return out[:, :B]


def _in_proj_kernel(x0_ref, xn_ref, shift0_ref, scale0_ref, shiftn_ref, scalen_ref, gpre_ref,
                    w_ref, cos_ref, sin_ref, z_ref, qt_ref, vt_ref, h_ref, *, tn):
    i = pl.program_id(0)
    n_col_tiles = w_ref.shape[1] // tn
    q_lo, k_lo, v_lo, gd_lo = (2 * D_POOL, 2 * D_POOL + D_QK, 2 * D_POOL + 2 * D_QK,
                               2 * D_POOL + 2 * D_QK + D_DIFF)

    def store_normed(x_ref, shift_ref, scale_ref, slot):
        gain = gpre_ref[...] * (1.0 + scale_ref[0])
        for r in range(0, x_ref.shape[0], NORM_ROWS):
            x = x_ref[r:r + NORM_ROWS, :]
            ms = jnp.mean(x * x, axis=-1, keepdims=True)
            h_ref[slot, r:r + NORM_ROWS, :] = (x * lax.rsqrt(ms + EPS) * gain
                                               + shift_ref[0]).astype(BF16)

    @pl.when(i == 0)
    def _():
        store_normed(x0_ref, shift0_ref, scale0_ref, 0)

    def step(cur):
        store_normed(xn_ref, shiftn_ref, scalen_ref, 1 - cur)
        h = h_ref[cur]
        z_col = 0
        for jt in range(n_col_tiles):
            col0 = jt * tn
            acc = jnp.dot(h, w_ref[:, col0:col0 + tn], preferred_element_type=F32)
            is_q, is_k, is_v = (q_lo <= col0 < k_lo), (k_lo <= col0 < v_lo), (v_lo <= col0 < gd_lo)
            if is_q or is_k:
                qk_scale = LOG2_E / math.sqrt(DIFF_QK_DIM) if is_q else 1.0
                cos = cos_ref[...] * qk_scale
                sin = sin_ref[...] * qk_scale
                for g in range(tn // LANES):
                    xg = acc[:, g * LANES:(g + 1) * LANES]
                    rot = pltpu.roll(xg, LANES // 2, 1)
                    roped = xg * cos + rot * sin
                    if is_q:
                        qt_ref[(col0 - q_lo) // LANES + g] = roped.T.astype(BF16)
                    else:
                        z_ref[:, z_col + g * LANES:z_col + (g + 1) * LANES] = roped.astype(BF16)
            elif is_v:
                for g in range(tn // LANES):
                    vt_ref[(col0 - v_lo) // LANES + g] = acc[:, g * LANES:(g + 1) * LANES].T.astype(BF16)
            else:
                z_ref[:, z_col:z_col + tn] = acc.astype(BF16)
            if not (is_q or is_v):
                z_col += tn

    for parity in range(2):
        pl.when(i % 2 == parity)(functools.partial(step, parity))


def _in_proj(x2, mod3, g_pre, w_in_bf, cos, sin, *, seq, layer):
    M, D = x2.shape
    N = w_in_bf.shape[2]
    tm, tn = 256, 1024
    per_seq = seq // tm
    last = M // tm - 1
    nxt = lambda i: jnp.minimum(i + 1, last)
    kern = functools.partial(_in_proj_kernel, tn=tn)
    return pl.pallas_call(
        kern,
        grid=(M // tm,),
        in_specs=[
            pl.BlockSpec((tm, D), lambda i: (0, 0)),
            pl.BlockSpec((tm, D), lambda i: (nxt(i), 0)),
            pl.BlockSpec((1, 1, D), lambda i: (0, 0, 0)),
            pl.BlockSpec((1, 1, D), lambda i: (0, 0, 1)),
            pl.BlockSpec((1, 1, D), lambda i: (nxt(i) // per_seq, 0, 0)),
            pl.BlockSpec((1, 1, D), lambda i: (nxt(i) // per_seq, 0, 1)),
            pl.BlockSpec((1, D), lambda i: (0, 0)),
            pl.BlockSpec((None, D, N), lambda i: (layer, 0, 0), pipeline_mode=pl.Buffered(1)),
            pl.BlockSpec((tm, LANES), lambda i: (i, 0)),
            pl.BlockSpec((tm, LANES), lambda i: (i, 0)),
        ],
        out_specs=[
            pl.BlockSpec((tm, D_Z), lambda i: (i, 0)),
            pl.BlockSpec((None, DIFF_HEADS, LANES, tm), lambda i: (i // per_seq, 0, 0, i % per_seq)),
            pl.BlockSpec((None, DIFF_HEADS, LANES, tm), lambda i: (i // per_seq, 0, 0, i % per_seq)),
        ],
        out_shape=[
            jax.ShapeDtypeStruct((M, D_Z), BF16),
            jax.ShapeDtypeStruct((M // seq, DIFF_HEADS, LANES, seq), BF16),
            jax.ShapeDtypeStruct((M // seq, DIFF_HEADS, LANES, seq), BF16),
        ],
        scratch_shapes=[pltpu.VMEM((2, tm, D), BF16)],
        compiler_params=pltpu.CompilerParams(
            dimension_semantics=("arbitrary",), vmem_limit_bytes=VMEM_LIMIT),
        name="in_proj",
    )(x2, x2, mod3, mod3, mod3, mod3, g_pre.reshape(1, D), w_in_bf, cos, sin)


def _attn_kernel(lq1_ref, lk1_ref, lq2_ref, lk2_ref, subg_ref, q_ref, k_ref, v_ref, g_ref,
                 o_ref, vt_ref, qcat_ref, acc_ref, *, seq, tq, lam_init):
    hq, nq = ATTN_GROUP, seq // tq
    n_parts = tq // hq
    n_groups = 2 * n_parts
    lam = (jnp.exp(jnp.sum(lq1_ref[...] * lk1_ref[...], axis=-1, keepdims=True))
           - jnp.exp(jnp.sum(lq2_ref[...] * lk2_ref[...], axis=-1, keepdims=True))
           + lam_init)

    dim = lax.broadcasted_iota(jnp.int32, (LANES, tq), 0)
    first_map = (dim // (DIFF_QK_DIM // 2)) % 2 == 0
    for c in range(nq):
        cols = slice(c * tq, (c + 1) * tq)
        vt_ref[:DIFF_V_DIM, cols] = v_ref[:, cols]
        q = q_ref[:, cols]
        zero = jnp.zeros_like(q)
        maps = (jnp.where(first_map, q, zero), jnp.where(first_map, zero, q))
        qcat_ref[c] = jnp.concatenate(
            [maps[g % 2][:, (g // 2) * hq:(g // 2 + 1) * hq] for g in range(n_groups)], axis=1)
    vt_ref[DIFF_V_DIM:, :] = jnp.ones((ONES_ROWS, seq), BF16)

    key_c = lax.broadcasted_iota(jnp.int32, (hq, hq), 0) // CHUNK
    row_c = lax.broadcasted_iota(jnp.int32, (hq, hq), 1) // CHUNK
    causal = key_c <= row_c

    def scores(tile):
        _, i, k0, c = tile
        return jnp.dot(k_ref[k0:k0 + hq, :], qcat_ref[i, :, c * hq:(c + 1) * hq],
                       preferred_element_type=F32)

    def softmax_pv(tile, s, m):
        masked, i, k0, c = tile
        acc = acc_ref.at[i % 2]
        cols = slice(c * hq, (c + 1) * hq)
        if masked:
            s = jnp.where(causal, s, NEG_INF)
        m_new = jnp.max(s, axis=0, keepdims=True)
        if m is not None:
            m_new = jnp.maximum(m, m_new)
        p = jnp.exp2(s - m_new).astype(BF16)
        pv = jnp.dot(vt_ref[:, k0:k0 + hq], p, preferred_element_type=F32)
        if m is None:
            acc[:, cols] = pv
        else:
            acc[:, cols] = acc[:, cols] * jnp.exp2(m - m_new) + pv
        return m_new

    def finish(i):
        acc = acc_ref.at[i % 2]
        o = acc[:DIFF_V_DIM, :] / acc[DIFF_V_DIM:DIFF_V_DIM + 1, :]
        by_map = [jnp.concatenate([o[:, (2 * r + mp) * hq:(2 * r + mp + 1) * hq]
                                   for r in range(n_parts)], axis=1) for mp in range(2)]
        att = (by_map[0] - lam * by_map[1]).T
        ms = jnp.mean(att * att, axis=-1, keepdims=True)
        y = att * lax.rsqrt(ms + EPS) * subg_ref[...] * (1.0 - lam_init)
        rows = slice(i * tq, (i + 1) * tq)
        o_ref[rows, :] = (y * _silu(g_ref[rows, :].astype(F32))).astype(BF16)

    tiles = []
    for i in range(nq):
        for k0 in range(0, i * tq, hq):
            tiles += [(False, i, k0, c) for c in range(n_groups)]
        for t in range(n_parts):
            tiles += [(c // 2 == t, i, i * tq + t * hq, c) for c in range(2 * t, n_groups)]
    pending = [scores(t) for t in tiles[:SCORE_LOOKAHEAD]]
    m = {}
    for n, tile in enumerate(tiles):
        if n + SCORE_LOOKAHEAD < len(tiles):
            pending.append(scores(tiles[n + SCORE_LOOKAHEAD]))
        _, i, _, c = tile
        m[c] = softmax_pv(tile, pending.pop(0), m.get(c))
        if n + 1 == len(tiles) or tiles[n + 1][1] != i:
            finish(i)
            m = {}


def _diff_attention(z3, q_t, v_t, lq1, lk1, lq2, lk2, subln_g, *, lam_init):
    B, S, _ = z3.shape
    tq = ATTN_TQ
    k_col = (2 * D_POOL) // LANES
    g_col = k_col + D_QK // LANES
    vec = lambda n: pl.BlockSpec((1, n), lambda b, h: (0, 0))
    head = lambda col: pl.BlockSpec((None, S, LANES), lambda b, h: (b, 0, col + h))
    head_t = pl.BlockSpec((None, None, LANES, S), lambda b, h: (b, h, 0, 0))
    kern = functools.partial(_attn_kernel, seq=S, tq=tq, lam_init=lam_init)
    return pl.pallas_call(
        kern,
        grid=(B, DIFF_HEADS),
        in_specs=[vec(DIFF_QK_DIM)] * 4 + [vec(DIFF_V_DIM),
                  head_t, head(k_col), head_t, head(g_col)],
        out_specs=pl.BlockSpec((None, S, LANES), lambda b, h: (b, 0, h)),
        out_shape=jax.ShapeDtypeStruct((B, S, D_DIFF), BF16),
        scratch_shapes=[pltpu.VMEM((DIFF_V_DIM + ONES_ROWS, S), BF16),
                        pltpu.VMEM((S // tq, LANES, 2 * tq), BF16),
                        pltpu.VMEM((2, DIFF_V_DIM + ONES_ROWS, 2 * tq), F32)],
        compiler_params=pltpu.CompilerParams(
            dimension_semantics=("parallel", "parallel"), vmem_limit_bytes=VMEM_LIMIT),
        name="diff_attention",
    )(lq1.reshape(1, -1), lk1.reshape(1, -1), lq2.reshape(1, -1), lk2.reshape(1, -1),
      subln_g.reshape(1, -1), q_t, z3, v_t, z3)


def _pool_bands(tm):
    lag = np.arange(tm)[:, None] - np.arange(tm)[None, :]
    hlag = np.arange(POOL_HALO)[:, None] + POOL_HALO - np.arange(POOL_HALO)[None, :]
    band = np.stack([(lag >= 0) & (lag < w) for w in POOL_WINDOWS]).astype(np.float32)
    hband = np.stack([hlag < w for w in POOL_WINDOWS]).astype(np.float32)
    return jnp.asarray(band, BF16), jnp.asarray(hband, BF16)


def _out_proj_kernel(u0_ref, gp0_ref, un_ref, halo_ref, gpn_ref, d_ref, band_ref, hband_ref, wp_ref,
                     ps_ref, w_ref, x_ref, gate_ref, gpost_ref, o_ref, mix_ref, y_ref, *, per_seq):
    i = pl.program_id(0)
    tm = band_ref.shape[1]
    gd = POOL_GROUP_DIM
    n_chunks = len(POOL_WINDOWS)
    cw = w_ref.shape[1] // n_chunks
    part = tm // n_chunks

    def pool_sums(u_ref_, g, halo):
        cols = slice(g * gd, (g + 1) * gd)
        win = jnp.dot(band_ref[g], u_ref_[:, cols], preferred_element_type=F32)
        if halo is None:
            return win
        halo_rows, seq_start = halo
        hwin = jnp.dot(hband_ref[g], halo_rows[:, cols], preferred_element_type=F32)
        hwin = jnp.where(seq_start, 0.0, hwin)
        return jnp.concatenate([win[:POOL_HALO] + hwin, win[POOL_HALO:]], axis=0)

    def pool_mix(u_ref_, gp_ref_, g, win, pos0, slot):
        cols = slice(g * gd, (g + 1) * gd)
        t1 = pos0 + lax.broadcasted_iota(jnp.int32, (tm, 1), 0) + 1
        count = jnp.minimum(t1, POOL_WINDOWS[g]).astype(F32)
        pooled = win / count - u_ref_[:, cols].astype(F32)
        mixed = (jnp.dot(pooled.astype(BF16), wp_ref[g], preferred_element_type=F32)
                 * ps_ref[:, cols])
        mix_ref[slot, :, cols] = (mixed * _silu(gp_ref_[:, cols].astype(F32))).astype(BF16)

    def epilogue_part(slot, n):
        rows = slice(n * part, (n + 1) * part)
        yq = [y_ref[slot, rows, c * cw:(c + 1) * cw] for c in range(n_chunks)]
        ms = sum(jnp.sum(v * v, axis=-1, keepdims=True) for v in yq) / w_ref.shape[1]
        scale = lax.rsqrt(ms + EPS)
        for c, v in enumerate(yq):
            cols = slice(c * cw, (c + 1) * cw)
            yn = v * scale * gpost_ref[:, cols]
            o_ref[rows, cols] = x_ref[rows, cols] + (1.0 + gate_ref[0, :, cols]) * yn

    @pl.when(i == 0)
    def _():
        for g in range(n_chunks):
            pool_mix(u0_ref, gp0_ref, g, pool_sums(u0_ref, g, None), 0, 0)
        y_ref[1] = jnp.zeros(y_ref.shape[1:], F32)

    def step(cur):
        nxt = 1 - cur
        it_next = (i + 1) % per_seq
        halo = (halo_ref, it_next == 0)
        mix_ref[cur, :, D_POOL:] = d_ref[...]
        def chunk(n):
            cols = slice(n * cw, (n + 1) * cw)
            y_ref[cur, :, cols] = jnp.dot(mix_ref[cur], w_ref[:, cols], preferred_element_type=F32)

        def mix(g, win):
            pool_mix(un_ref, gpn_ref, g, win, it_next * tm, nxt)

        win = {g: pool_sums(un_ref, g, halo) for g in (0, 1)}
        chunk(0)
        epilogue_part(nxt, 0)
        mix(0, win[0])
        epilogue_part(nxt, 1)
        win[2] = pool_sums(un_ref, 2, halo)
        chunk(1)
        mix(1, win[1])
        epilogue_part(nxt, 2)
        win[3] = pool_sums(un_ref, 3, halo)
        chunk(2)
        mix(2, win[2])
        epilogue_part(nxt, 3)
        mix(3, win[3])
        chunk(3)

    for parity in range(2):
        pl.when(i % 2 == parity)(functools.partial(step, parity))


def _out_proj(z2, diff_out, w_pool_bf, pool_scale, w_out_bf, x2, mod3, g_post, *, seq, layer):
    M, D = x2.shape
    tm = 256
    per_seq = seq // tm
    n_tiles = M // tm
    band, hband = _pool_bands(tm)
    const = lambda shape: pl.BlockSpec(shape, lambda i: (0,) * len(shape))
    of_layer = lambda shape, **kw: pl.BlockSpec((None,) + shape,
                                                lambda i: (layer,) + (0,) * len(shape), **kw)
    cur = lambda i: jnp.minimum(i, n_tiles - 1)
    nxt = lambda i: jnp.minimum(i + 1, n_tiles - 1)
    prv = lambda i: jnp.maximum(i - 1, 0)
    kern = functools.partial(_out_proj_kernel, per_seq=per_seq)
    return pl.pallas_call(
        kern,
        grid=(n_tiles + 1,),
        in_specs=[
            pl.BlockSpec((tm, D_POOL), lambda i: (0, 0)),
            pl.BlockSpec((tm, D_POOL), lambda i: (0, 1)),
            pl.BlockSpec((tm, D_POOL), lambda i: (nxt(i), 0)),
            pl.BlockSpec((POOL_HALO, D_POOL),
                         lambda i: (nxt(i) * (tm // POOL_HALO) - 1, 0)),
            pl.BlockSpec((tm, D_POOL), lambda i: (nxt(i), 1)),
            pl.BlockSpec((tm, D_DIFF), lambda i: (cur(i), 0)),
            const(band.shape), const(hband.shape), of_layer(w_pool_bf.shape[1:]),
            const((1, D_POOL)), of_layer((D, D), pipeline_mode=pl.Buffered(1)),
            pl.BlockSpec((tm, D), lambda i: (prv(i), 0)),
            pl.BlockSpec((1, 1, D), lambda i: (prv(i) // per_seq, 0, 2)),
            const((1, D)),
        ],
        out_specs=pl.BlockSpec((tm, D), lambda i: (prv(i), 0)),
        out_shape=jax.ShapeDtypeStruct((M, D), F32),
        scratch_shapes=[pltpu.VMEM((2, tm, D), BF16),
                        pltpu.VMEM((2, tm, D), F32)],
        compiler_params=pltpu.CompilerParams(
            dimension_semantics=("arbitrary",), vmem_limit_bytes=VMEM_LIMIT),
        name="out_proj",
    )(z2, z2, z2, z2, z2, diff_out, band, hband, w_pool_bf, pool_scale.reshape(1, D_POOL),
      w_out_bf, x2, mod3, g_post.reshape(1, D))


def _w_in_prep_kernel(w_ref, o_ref, *, tn):
    j = pl.program_id(1)
    first_qk = (2 * D_POOL) // tn
    is_qk = jnp.logical_and(j >= first_qk, j < first_qk + (2 * D_QK) // tn)

    @pl.when(is_qk)
    def _():
        quarter = DIFF_QK_DIM // 2
        blk = lax.broadcasted_iota(jnp.int32, (w_ref.shape[0], LANES), 1) // quarter
        for g in range(tn // LANES):
            x = w_ref[:, g * LANES:(g + 1) * LANES]
            y = jnp.where(blk == 1, pltpu.roll(x, LANES - quarter, 1),
                          jnp.where(blk == 2, pltpu.roll(x, quarter, 1), x))
            o_ref[:, g * LANES:(g + 1) * LANES] = y.astype(BF16)

    @pl.when(jnp.logical_not(is_qk))
    def _():
        o_ref[...] = w_ref[...].astype(BF16)


def _w_in_prep(w_in):
    depth, d, n = w_in.shape
    tn = 512
    return pl.pallas_call(
        functools.partial(_w_in_prep_kernel, tn=tn),
        grid=(depth, n // tn),
        in_specs=[pl.BlockSpec((None, d, tn), lambda l, j: (l, 0, j))],
        out_specs=pl.BlockSpec((None, d, tn), lambda l, j: (l, 0, j)),
        out_shape=jax.ShapeDtypeStruct((depth, d, n), BF16),
        compiler_params=pltpu.CompilerParams(
            dimension_semantics=("parallel", "parallel"), vmem_limit_bytes=VMEM_LIMIT),
        name="w_in_prep",
    )(w_in)


def kernel(x, c, positions, w_ada, b_ada, g_pre, w_in, w_pool, pool_scale, lambda_q1, lambda_k1,
           lambda_q2, lambda_k2, subln_g, w_out, g_post):
    B, S, D = x.shape
    depth = w_in.shape[0]
    cos, sin = _rope_tables(positions)
    mod = _ada_modulation(c, w_ada, b_ada)
    w_in_bf = _w_in_prep(w_in)
    w_pool_bf = w_pool.astype(BF16)
    w_out_bf = w_out.astype(BF16)
    x2 = x.reshape(B * S, D)
    for l in range(depth):
        lam_init = 0.8 - 0.6 * math.exp(-0.3 * l)
        mod3 = mod[l].reshape(B, 1, 3 * D)
        z2, q_t, v_t = _in_proj(x2, mod3, g_pre[l], w_in_bf, cos, sin, seq=S, layer=l)
        diff_out = _diff_attention(z2.reshape(B, S, D_Z), q_t, v_t, lambda_q1[l], lambda_k1[l],
                                   lambda_q2[l], lambda_k2[l], subln_g[l], lam_init=lam_init)
        x2 = _out_proj(z2, diff_out.reshape(B * S, D_DIFF), w_pool_bf, pool_scale[l],
                       w_out_bf, x2, mod3, g_post[l], seq=S, layer=l)
    return x2.reshape(B, S, D)
```
